```python
import jax, jax.numpy as jnp
from jax import lax
import numpy as np

D_MODEL = 2048
BATCH = 4
SEQ = 2048
DEPTH = 1
DEC_BATCH = 128
DEC_SEQ = 1
PAST_LEN = 2048
PAGE_SIZE = 128

HEAD_DIM = 128
N_HEADS = D_MODEL // HEAD_DIM
MOBA_HEADS = N_HEADS // 2
NSA_HEADS = N_HEADS - MOBA_HEADS
NSA_KV_HEADS = 2
NSA_GROUP = NSA_HEADS // NSA_KV_HEADS
MOBA_W = MOBA_HEADS * HEAD_DIM
NSA_W = NSA_HEADS * HEAD_DIM
NSA_KVW = NSA_KV_HEADS * HEAD_DIM
MOBA_BLOCK = 256
MOBA_TOPK = 3
NSA_CMP_LEN = 32
NSA_CMP_STRIDE = 16
NSA_CMP_HIDDEN = 256
NSA_SEL_BLOCK = 64
NSA_SEL_TOPK = 4
NSA_WINDOW = 512
NSA_BRANCHES = 3
PEER_KEYS = 128
PEER_EXPERTS = PEER_KEYS * PEER_KEYS
PEER_HEADS = 8
PEER_QDIM = 256
PEER_TOPK = 16
PEER_TOKEN_BLOCK = 128
QUERY_BLOCK = 128
N_MOD = 6
RMS_EPS = 1e-6
IN_SIZES = (MOBA_W, MOBA_W, MOBA_W, NSA_W, NSA_KVW, NSA_KVW, NSA_KVW, NSA_KVW, NSA_KVW, NSA_KVW, NSA_HEADS * NSA_BRANCHES)

kernel_name = 'hymba_moba_nsa_peer_step'


def rmsnorm(x, g):
    xf = x.astype(jnp.float32)
    y = xf * lax.rsqrt(jnp.mean(xf * xf, axis=-1, keepdims=True) + RMS_EPS)
    return (y * g.astype(jnp.float32)).astype(x.dtype)


def alibi_slopes(n):
    return jnp.asarray(2.0 ** (-8.0 * np.arange(1, n + 1) / n), dtype=jnp.float32)


def masked_softmax(s, mask):
    s = jnp.where(mask, s.astype(jnp.float32), -jnp.inf)
    m = jnp.max(s, axis=-1, keepdims=True)
    m = jnp.where(jnp.isfinite(m), m, 0.0)
    p = jnp.where(mask, jnp.exp(s - m), 0.0)
    return p / jnp.maximum(jnp.sum(p, axis=-1, keepdims=True), 1e-30)


def adaln_mods(c, w_ada, b_ada):
    mod = jax.nn.silu(c) @ w_ada + b_ada
    return jnp.split(mod[:, None, :], N_MOD, axis=-1)


def modulate(x, g, shift, scale):
    return rmsnorm(x, g) * (1.0 + scale) + shift


def compress_tokens(x, pe, w1, w2):
    tk, g, dh = x.shape
    nc = (tk - NSA_CMP_LEN) // NSA_CMP_STRIDE + 1
    idx = NSA_CMP_STRIDE * jnp.arange(nc)[:, None] + jnp.arange(NSA_CMP_LEN)[None, :]
    blk = x[idx] + pe[None, :, None, :]
    flat = blk.transpose(0, 2, 1, 3).reshape(nc, g, NSA_CMP_LEN * dh)
    return jax.nn.gelu(flat @ w1) @ w2


def mix_sequence(q_m, q_n, gate_n, kv_m, kv_c, kv_s, kv_w, pos0, cmp_k, cmp_v):
    f32 = jnp.float32
    tq = q_m.shape[0]
    tk = kv_m.shape[0]
    G, R, dh, HM = NSA_KV_HEADS, NSA_GROUP, HEAD_DIM, MOBA_HEADS
    MB, LS, W = MOBA_BLOCK, NSA_SEL_BLOCK, NSA_WINDOW
    scale = HEAD_DIM ** -0.5
    slope_m = alibi_slopes(HM)
    slope_n = alibi_slopes(NSA_HEADS).reshape(G, R)

    k_m, v_m = kv_m[:, 0], kv_m[:, 1]
    nb_m = tk // MB
    nsel_m = min(MOBA_TOPK, nb_m)
    pad_m = ((0, (-tk) % MB), (0, 0), (0, 0))
    k_m_pad, v_m_pad = jnp.pad(k_m, pad_m), jnp.pad(v_m, pad_m)
    kb_m = k_m[:nb_m * MB].reshape(nb_m, MB, HM, dh).transpose(2, 0, 1, 3)
    vb_m = v_m[:nb_m * MB].reshape(nb_m, MB, HM, dh).transpose(2, 0, 1, 3)
    k_mean = jnp.mean(kb_m.astype(f32), axis=2).astype(k_m.dtype)

    ck = compress_tokens(kv_c[:, 0], *cmp_k)
    cv = compress_tokens(kv_c[:, 1], *cmp_v)
    c_start = NSA_CMP_STRIDE * jnp.arange(ck.shape[0])
    c_end = c_start + NSA_CMP_LEN - 1
    nb_s = tk // LS
    nsel_s = min(NSA_SEL_TOPK, nb_s)
    b_start = LS * jnp.arange(nb_s)
    overlap = ((c_start[:, None] < b_start[None, :] + LS) & (c_end[:, None] >= b_start[None, :])).astype(f32)
    k_s, v_s = kv_s[:, 0], kv_s[:, 1]
    pad_s = ((0, (-tk) % LS), (0, 0), (0, 0))
    k_s_pad, v_s_pad = jnp.pad(k_s, pad_s), jnp.pad(v_s, pad_s)
    kb_s = k_s[:nb_s * LS].reshape(nb_s, LS, G, dh).transpose(2, 0, 1, 3)
    vb_s = v_s[:nb_s * LS].reshape(nb_s, LS, G, dh).transpose(2, 0, 1, 3)

    qb = QUERY_BLOCK if tq % QUERY_BLOCK == 0 else tq

    def moba_block(qm, t):
        nq = qm.shape[0]
        own_idx = (t // MB * MB)[:, None] + jnp.arange(MB)[None, :]
        k_own, v_own = k_m_pad[own_idx], v_m_pad[own_idx]
        s_own = (jnp.einsum('qhd,qnhd->qhn', qm, k_own).astype(f32) * scale
                 - slope_m[None, :, None] * (t[:, None] - own_idx).astype(f32)[:, None, :])
        m_own = jnp.broadcast_to((own_idx <= t[:, None])[:, None, :], s_own.shape)
        if nsel_m == 0:
            p = masked_softmax(s_own, m_own).astype(v_own.dtype)
            return jnp.einsum('qhn,qnhd->qhd', p, v_own)
        gate = jnp.einsum('qhd,hjd->qhj', qm, k_mean).astype(f32)
        gate = jnp.where(jnp.arange(nb_m)[None, None, :] < (t // MB)[:, None, None], gate, -jnp.inf)
        g_val, sel = lax.top_k(gate, nsel_m)
        hidx = jnp.arange(HM)[None, :, None]
        k_sel, v_sel = kb_m[hidx, sel], vb_m[hidx, sel]
        s_pos = sel[..., None] * MB + jnp.arange(MB)
        s_sel = (jnp.einsum('qhd,qhsnd->qhsn', qm, k_sel).astype(f32) * scale
                 - slope_m[None, :, None, None] * (t[:, None, None, None] - s_pos).astype(f32))
        m_sel = jnp.broadcast_to(jnp.isfinite(g_val)[..., None], s_sel.shape)
        p = masked_softmax(jnp.concatenate([s_sel.reshape(nq, HM, -1), s_own], -1),
                           jnp.concatenate([m_sel.reshape(nq, HM, -1), m_own], -1)).astype(v_own.dtype)
        p_sel = p[..., :nsel_m * MB].reshape(s_sel.shape)
        p_own = p[..., nsel_m * MB:]
        return jnp.einsum('qhsn,qhsnd->qhd', p_sel, v_sel) + jnp.einsum('qhn,qnhd->qhd', p_own, v_own)

    def nsa_block(qn, gn, t, i0):
        nq = qn.shape[0]
        qg = qn.reshape(nq, G, R, dh)
        s_c = jnp.einsum('qgrd,cgd->qgrc', qg, ck).astype(f32) * scale
        m_c = jnp.broadcast_to((c_end[None, :] <= t[:, None])[:, None, None, :], s_c.shape)
        p_c = masked_softmax(s_c, m_c)
        o_c = jnp.einsum('qgrc,cgd->qgrd', p_c.astype(cv.dtype), cv)
        own_idx = (t // LS * LS)[:, None] + jnp.arange(LS)[None, :]
        k_own, v_own = k_s_pad[own_idx], v_s_pad[own_idx]
        s_own = (jnp.einsum('qgrd,qngd->qgrn', qg, k_own).astype(f32) * scale
                 - slope_n[None, :, :, None] * (t[:, None] - own_idx).astype(f32)[:, None, None, :])
        m_own = jnp.broadcast_to((own_idx <= t[:, None])[:, None, None, :], s_own.shape)
        imp = jnp.einsum('qgc,cj->qgj', jnp.sum(p_c, axis=2), overlap)
        imp = jnp.where(jnp.arange(nb_s)[None, None, :] < (t // LS)[:, None, None], imp, -jnp.inf)
        i_val, sel = lax.top_k(imp, nsel_s)
        gidx = jnp.arange(G)[None, :, None]
        k_sel, v_sel = kb_s[gidx, sel], vb_s[gidx, sel]
        s_pos = sel[..., None] * LS + jnp.arange(LS)
        s_sel = (jnp.einsum('qgrd,qgsnd->qgrsn', qg, k_sel).astype(f32) * scale
                 - slope_n[None, :, :, None, None] * (t[:, None, None, None] - s_pos).astype(f32)[:, :, None])
        m_sel = jnp.broadcast_to(jnp.isfinite(i_val)[:, :, None, :, None], s_sel.shape)
        p = masked_softmax(jnp.concatenate([s_sel.reshape(nq, G, R, -1), s_own], -1),
                           jnp.concatenate([m_sel.reshape(nq, G, R, -1), m_own], -1)).astype(v_own.dtype)
        o_s = (jnp.einsum('qgrsn,qgsnd->qgrd', p[..., :nsel_s * LS].reshape(s_sel.shape), v_sel)
               + jnp.einsum('qgrn,qngd->qgrd', p[..., nsel_s * LS:], v_own))
        band = lax.dynamic_slice_in_dim(kv_w, i0, W + nq, 0)
        b_pos = pos0 - W + i0 + jnp.arange(W + nq)
        s_w = (jnp.einsum('qgrd,ngd->qgrn', qg, band[:, 0]).astype(f32) * scale
               - slope_n[None, :, :, None] * (t[:, None] - b_pos[None, :]).astype(f32)[:, None, None, :])
        m_w = (b_pos[None, :] <= t[:, None]) & (b_pos[None, :] > t[:, None] - W) & (b_pos[None, :] >= 0)
        p_w = masked_softmax(s_w, jnp.broadcast_to(m_w[:, None, None, :], s_w.shape)).astype(band.dtype)
        o_w = jnp.einsum('qgrn,ngd->qgrd', p_w, band[:, 1])
        g = gn.reshape(nq, G, R, NSA_BRANCHES)
        return g[..., 0:1] * o_c + g[..., 1:2] * o_s + g[..., 2:3] * o_w

    def chunk(ci):
        i0 = ci * qb
        qm = lax.dynamic_slice_in_dim(q_m, i0, qb, 0)
        qn = lax.dynamic_slice_in_dim(q_n, i0, qb, 0)
        gn = lax.dynamic_slice_in_dim(gate_n, i0, qb, 0)
        t = pos0 + i0 + jnp.arange(qb)
        o_m = moba_block(qm, t).reshape(qb, MOBA_W).astype(q_m.dtype)
        o_n = nsa_block(qn, gn, t, i0).reshape(qb, NSA_W).astype(q_m.dtype)
        return jnp.concatenate([o_m, o_n], axis=-1)

    out = lax.map(chunk, jnp.arange(tq // qb))
    return out.reshape(tq, MOBA_W + NSA_W)


def mixer_inputs(x, mods, g_norm1, w_in):
    b, t, _ = x.shape
    h = modulate(x, g_norm1, mods[0], mods[1])
    split_at = np.cumsum(IN_SIZES)[:-1].tolist()
    qm, km, vm, qn, kc, vc, ks, vs, kw, vw, gl = jnp.split(h @ w_in, split_at, axis=-1)
    kv = lambda k, v, nh: jnp.stack([k, v], axis=2).reshape(b, t, 2, nh, HEAD_DIM)
    gates = jax.nn.sigmoid(gl.astype(jnp.float32)).astype(x.dtype).reshape(b, t, NSA_HEADS, NSA_BRANCHES)
    return (qm.reshape(b, t, MOBA_HEADS, HEAD_DIM), qn.reshape(b, t, NSA_HEADS, HEAD_DIM), gates,
            kv(km, vm, MOBA_HEADS), kv(kc, vc, NSA_KV_HEADS), kv(ks, vs, NSA_KV_HEADS), kv(kw, vw, NSA_KV_HEADS))


def peer_ffn(h, w_q, keys, u, v):
    shp = h.shape
    xt = h.reshape(-1, D_MODEL)
    n = xt.shape[0]
    xt = jnp.pad(xt, ((0, (-n) % PEER_TOKEN_BLOCK), (0, 0)))
    xb = xt.reshape(-1, PEER_TOKEN_BLOCK, D_MODEL)

    def block(xc):
        nt = xc.shape[0]
        q = (xc @ w_q).reshape(nt, PEER_HEADS, 2, PEER_QDIM // 2)
        s = jnp.einsum('nhcd,hckd->nhck', q, keys).astype(jnp.float32)
        sv, si = lax.top_k(s, PEER_TOPK)
        cand = (sv[:, :, 0, :, None] + sv[:, :, 1, None, :]).reshape(nt, PEER_HEADS, -1)
        cidx = (si[:, :, 0, :, None] * PEER_KEYS + si[:, :, 1, None, :]).reshape(nt, PEER_HEADS, -1)
        fv, fi = lax.top_k(cand, PEER_TOPK)
        eidx = jnp.take_along_axis(cidx, fi, axis=-1)
        g = jax.nn.softmax(fv, axis=-1)
        ue, ve = u[eidx], v[eidx]
        a = jax.nn.gelu(jnp.einsum('nd,nhkd->nhk', xc, ue).astype(jnp.float32))
        return jnp.einsum('nhk,nhkd->nd', (g * a).astype(xc.dtype), ve)

    out = lax.map(block, xb).reshape(-1, D_MODEL)[:n]
    return out.reshape(shp)


def block_output(x, o, mods, w_out, g_norm2, peer_w_q, peer_keys, peer_u, peer_v):
    x = x + mods[2] * (o @ w_out)
    h = modulate(x, g_norm2, mods[3], mods[4])
    return x + mods[5] * peer_ffn(h, peer_w_q, peer_keys, peer_u, peer_v)


def setup_inputs(seed: int = 0) -> dict:
    key = jax.random.key(seed)
    ks = jax.random.split(key, 32)
    f32 = jnp.float32
    n_pages = PAST_LEN // PAGE_SIZE
    n_used = DEC_BATCH * n_pages
    n_pool = n_used + max(1, n_used // 4)
    wb = min(NSA_WINDOW, PAST_LEN)
    p_in = sum(IN_SIZES)
    nrm = lambda k, shape, s: s * jax.random.normal(k, shape, f32)
    page_table = jax.random.permutation(ks[0], n_pool)[:n_used].reshape(DEC_BATCH, n_pages).astype(jnp.int32)
    return {
        'x_prompt': nrm(ks[1], (BATCH, SEQ, D_MODEL), 1.0),
        'x_sample': nrm(ks[2], (DEC_BATCH, DEC_SEQ, D_MODEL), 1.0),
        'cache_moba_kv': nrm(ks[3], (DEPTH, n_pool, PAGE_SIZE, 2, MOBA_HEADS, HEAD_DIM), 1.0),
        'cache_nsa_cmp_kv': nrm(ks[4], (DEPTH, n_pool, PAGE_SIZE, 2, NSA_KV_HEADS, HEAD_DIM), 1.0),
        'cache_nsa_slc_kv': nrm(ks[5], (DEPTH, n_pool, PAGE_SIZE, 2, NSA_KV_HEADS, HEAD_DIM), 1.0),
        'state_nsa_win_kv': nrm(ks[6], (DEPTH, DEC_BATCH, wb, 2, NSA_KV_HEADS, HEAD_DIM), 1.0),
        'page_table': page_table,
        'c_prompt': nrm(ks[7], (BATCH, D_MODEL), 1.0),
        'c_sample': nrm(ks[8], (DEC_BATCH, D_MODEL), 1.0),
        'w_ada': nrm(ks[9], (DEPTH, D_MODEL, N_MOD * D_MODEL), 0.5 * D_MODEL ** -0.5),
        'b_ada': nrm(ks[10], (DEPTH, N_MOD * D_MODEL), 0.01),
        'g_norm1': 1.0 + nrm(ks[11], (DEPTH, D_MODEL), 0.05),
        'w_in': nrm(ks[12], (DEPTH, D_MODEL, p_in), D_MODEL ** -0.5),
        'cmp_pe_k': nrm(ks[13], (DEPTH, NSA_CMP_LEN, HEAD_DIM), 0.1),
        'cmp_w1_k': nrm(ks[14], (DEPTH, NSA_CMP_LEN * HEAD_DIM, NSA_CMP_HIDDEN), (NSA_CMP_LEN * HEAD_DIM) ** -0.5),
        'cmp_w2_k': nrm(ks[15], (DEPTH, NSA_CMP_HIDDEN, HEAD_DIM), NSA_CMP_HIDDEN ** -0.5),
        'cmp_pe_v': nrm(ks[16], (DEPTH, NSA_CMP_LEN, HEAD_DIM), 0.1),
        'cmp_w1_v': nrm(ks[17], (DEPTH, NSA_CMP_LEN * HEAD_DIM, NSA_CMP_HIDDEN), (NSA_CMP_LEN * HEAD_DIM) ** -0.5),
        'cmp_w2_v': nrm(ks[18], (DEPTH, NSA_CMP_HIDDEN, HEAD_DIM), NSA_CMP_HIDDEN ** -0.5),
        'w_out': nrm(ks[19], (DEPTH, MOBA_W + NSA_W, D_MODEL), (MOBA_W + NSA_W) ** -0.5),
        'g_norm2': 1.0 + nrm(ks[20], (DEPTH, D_MODEL), 0.05),
        'peer_w_q': nrm(ks[21], (DEPTH, D_MODEL, PEER_HEADS * PEER_QDIM), D_MODEL ** -0.5),
        'peer_keys': nrm(ks[22], (DEPTH, PEER_HEADS, 2, PEER_KEYS, PEER_QDIM // 2), (PEER_QDIM // 2) ** -0.5),
        'peer_u': nrm(ks[23], (DEPTH, PEER_EXPERTS, D_MODEL), D_MODEL ** -0.5),
        'peer_v': nrm(ks[24], (DEPTH, PEER_EXPERTS, D_MODEL), PEER_HEADS ** -0.5),
        'g_final': 1.0 + nrm(ks[25], (D_MODEL,), 0.05),
    }


def reference(x_prompt, x_sample, cache_moba_kv, cache_nsa_cmp_kv, cache_nsa_slc_kv, state_nsa_win_kv, page_table,
              c_prompt, c_sample, w_ada, b_ada, g_norm1, w_in, cmp_pe_k, cmp_w1_k, cmp_w2_k, cmp_pe_v, cmp_w1_v,
              cmp_w2_v, w_out, g_norm2, peer_w_q, peer_keys, peer_u, peer_v, g_final):
    n_pages = page_table.shape[1]
    past = n_pages * PAGE_SIZE
    wb = state_nsa_win_kv.shape[2]
    seq = x_prompt.shape[1]
    xp, xs = x_prompt, x_sample
    moba_p, moba_s, cmp_p, cmp_s, slc_p, slc_s, win_p, win_s = [], [], [], [], [], [], [], []
    for l in range(DEPTH):
        cmp_k = (cmp_pe_k[l], cmp_w1_k[l], cmp_w2_k[l])
        cmp_v = (cmp_pe_v[l], cmp_w1_v[l], cmp_w2_v[l])

        mods_p = adaln_mods(c_prompt, w_ada[l], b_ada[l])
        qm, qn, gn, kvm, kvc, kvs, kvw = mixer_inputs(xp, mods_p, g_norm1[l], w_in[l])
        kvw_ext = jnp.pad(kvw, ((0, 0), (NSA_WINDOW, 0), (0, 0), (0, 0), (0, 0)))
        o_p = lax.map(lambda a: mix_sequence(*a, 0, cmp_k, cmp_v), (qm, qn, gn, kvm, kvc, kvs, kvw_ext))
        moba_p.append(kvm)
        cmp_p.append(kvc)
        slc_p.append(kvs)
        win_p.append(kvw[:, seq - min(NSA_WINDOW, seq):])

        mods_s = adaln_mods(c_sample, w_ada[l], b_ada[l])
        sqm, sqn, sgn, skvm, skvc, skvs, skvw = mixer_inputs(xs, mods_s, g_norm1[l], w_in[l])
        buf = state_nsa_win_kv[l]

        def sample_seq(a):
            qm_, qn_, gn_, kvm_, kvc_, kvs_, kvw_, pages, buf_ = a

            def gather_past(cache):
                rows = cache[l, pages]
                return rows.reshape((past,) + rows.shape[2:])

            kvw_ext_ = jnp.concatenate([jnp.zeros((NSA_WINDOW - wb,) + buf_.shape[1:], buf_.dtype), buf_, kvw_], 0)
            return mix_sequence(qm_, qn_, gn_,
                                jnp.concatenate([gather_past(cache_moba_kv), kvm_], 0),
                                jnp.concatenate([gather_past(cache_nsa_cmp_kv), kvc_], 0),
                                jnp.concatenate([gather_past(cache_nsa_slc_kv), kvs_], 0),
                                kvw_ext_, past, cmp_k, cmp_v)

        o_s = lax.map(sample_seq, (sqm, sqn, sgn, skvm, skvc, skvs, skvw, page_table, buf))
        moba_s.append(skvm)
        cmp_s.append(skvc)
        slc_s.append(skvs)
        win_s.append(jnp.concatenate([buf, skvw], axis=1)[:, skvw.shape[1]:])

        xp = block_output(xp, o_p, mods_p, w_out[l], g_norm2[l], peer_w_q[l], peer_keys[l], peer_u[l], peer_v[l])
        xs = block_output(xs, o_s, mods_s, w_out[l], g_norm2[l], peer_w_q[l], peer_keys[l], peer_u[l], peer_v[l])

    y_prompt = rmsnorm(xp, g_final)
    y_sample = rmsnorm(xs, g_final)
    return (y_prompt, y_sample, jnp.stack(moba_p), jnp.stack(moba_s), jnp.stack(cmp_p), jnp.stack(cmp_s),
            jnp.stack(slc_p), jnp.stack(slc_s), jnp.stack(win_p), jnp.stack(win_s))
```

```python
import functools

import jax
import jax.numpy as jnp
import numpy as np
from jax import lax
from jax.experimental import pallas as pl
from jax.experimental.pallas import tpu as pltpu

f32 = jnp.float32
bf16 = jnp.bfloat16
HIGHEST = lax.Precision.HIGHEST

D_MODEL = 2048
HEAD_DIM = 128
MOBA_HEADS = 8
NSA_HEADS = 8
NSA_KV_HEADS = 2
NSA_GROUP = 4
MOBA_W = MOBA_HEADS * HEAD_DIM
NSA_W = NSA_HEADS * HEAD_DIM
NSA_KVW = NSA_KV_HEADS * HEAD_DIM
MOBA_BLOCK = 256
MOBA_TOPK = 3
NSA_CMP_LEN = 32
NSA_CMP_STRIDE = 16
NSA_CMP_HIDDEN = 256
NSA_SEL_BLOCK = 64
NSA_SEL_TOPK = 4
NSA_WINDOW = 512
NSA_BRANCHES = 3
PEER_KEYS = 128
PEER_HEADS = 8
PEER_QDIM = 256
PEER_TOPK = 16
N_MOD = 6
RMS_EPS = 1e-6
PAGE_SIZE = 128
SCALE = HEAD_DIM ** -0.5
NEG = -jnp.inf
LANES = 128

IN_TILE = 512
IN_COLS = 12 * IN_TILE
VMEM_LIMIT = 56 * 1024 * 1024


def _cparams(sem):
    return pltpu.CompilerParams(dimension_semantics=sem, vmem_limit_bytes=VMEM_LIMIT)


def _gelu(x):
    return 0.5 * x * (1.0 + jnp.tanh(np.sqrt(2.0 / np.pi).astype(np.float32) * (x + 0.044715 * (x * x * x))))


def _dot_nt(a, b, precision=None):
    return lax.dot_general(a, b, (((1,), (1,)), ((), ())), precision=precision, preferred_element_type=f32)


def _dot(a, b, precision=None):
    return jnp.dot(a, b, precision=precision, preferred_element_type=f32)


def _ada_kernel(c_ref, w_ref, b_ref, o_ref):
    c = c_ref[...]
    a = c * jax.nn.sigmoid(c)
    o_ref[...] = _dot(a, w_ref[...], HIGHEST) + b_ref[...]


def ada_mods(c, w_ada, b_ada):
    rows = c.shape[0]
    n = w_ada.shape[1]
    tn = 1024
    return pl.pallas_call(
        _ada_kernel,
        grid=(n // tn,),
        in_specs=[pl.BlockSpec((rows, D_MODEL), lambda j: (0, 0)),
                  pl.BlockSpec((D_MODEL, tn), lambda j: (0, j)),
                  pl.BlockSpec((1, tn), lambda j: (0, j))],
        out_specs=pl.BlockSpec((rows, tn), lambda j: (0, j)),
        out_shape=jax.ShapeDtypeStruct((rows, n), f32),
        compiler_params=_cparams(("arbitrary",)),
        name="ada_mods",
    )(c, w_ada, b_ada.reshape(1, n))


def _rms_mod(x, g, shift, scale):
    y = x * lax.rsqrt(jnp.mean(x * x, axis=-1, keepdims=True) + RMS_EPS)
    return (y * g) * (1.0 + scale) + shift


def _inproj_kernel(x_ref, g_ref, sh_ref, sc_ref, w_ref,
                   qm_ref, kvm_ref, qn_ref, kvc_ref, kvs_ref, kvw_ref, gt_ref, h_scr):
    j = pl.program_id(1)

    @pl.when(j == 0)
    def _():
        h_scr[...] = _rms_mod(x_ref[...], g_ref[...], sh_ref[0], sc_ref[0]).astype(bf16)

    acc = _dot(h_scr[...], w_ref[...])

    @pl.when(j < 2)
    def _():
        qm_ref[...] = acc

    @pl.when((j >= 2) & (j < 6))
    def _():
        kvm_ref[...] = acc

    @pl.when((j >= 6) & (j < 8))
    def _():
        qn_ref[...] = acc

    @pl.when(j == 8)
    def _():
        kvc_ref[...] = acc

    @pl.when(j == 9)
    def _():
        kvs_ref[...] = acc

    @pl.when(j == 10)
    def _():
        kvw_ref[...] = acc

    @pl.when(j == 11)
    def _():
        gt_ref[...] = jax.nn.sigmoid(acc[:, :LANES])


def in_projection(x, mods3, g_norm1, w_in_p, tm, rows_per_mod):
    t = x.shape[0]
    r = mods3.shape[1]
    tiles_per_mod = rows_per_mod // tm
    mod_spec = lambda which: pl.BlockSpec((1, r, D_MODEL), lambda i, j: (i // tiles_per_mod, 0, which))
    clip = lambda j, lo, n: jnp.clip(j - lo, 0, n - 1)
    out_shapes = [jax.ShapeDtypeStruct((t, w), f32) for w in (MOBA_W, 2 * MOBA_W, NSA_W, 2 * NSA_KVW, 2 * NSA_KVW, 2 * NSA_KVW, LANES)]
    out_specs = [
        pl.BlockSpec((tm, IN_TILE), lambda i, j: (i, clip(j, 0, 2))),
        pl.BlockSpec((tm, IN_TILE), lambda i, j: (i, clip(j, 2, 4))),
        pl.BlockSpec((tm, IN_TILE), lambda i, j: (i, clip(j, 6, 2))),
        pl.BlockSpec((tm, IN_TILE), lambda i, j: (i, 0)),
        pl.BlockSpec((tm, IN_TILE), lambda i, j: (i, 0)),
        pl.BlockSpec((tm, IN_TILE), lambda i, j: (i, 0)),
        pl.BlockSpec((tm, LANES), lambda i, j: (i, 0)),
    ]
    return pl.pallas_call(
        _inproj_kernel,
        grid=(t // tm, IN_COLS // IN_TILE),
        in_specs=[pl.BlockSpec((tm, D_MODEL), lambda i, j: (i, 0)),
                  pl.BlockSpec((1, D_MODEL), lambda i, j: (0, 0)),
                  mod_spec(0), mod_spec(1),
                  pl.BlockSpec((D_MODEL, IN_TILE), lambda i, j: (0, j))],
        out_specs=out_specs,
        out_shape=out_shapes,
        scratch_shapes=[pltpu.VMEM((tm, D_MODEL), bf16)],
        compiler_params=_cparams(("arbitrary", "arbitrary")),
        name="in_projection",
    )(x, g_norm1.reshape(1, D_MODEL), mods3, mods3, w_in_p)


def _topk_mask(score, k):
    lane = lax.broadcasted_iota(jnp.int32, score.shape, 1)
    sel = jnp.zeros(score.shape, f32)
    g = score
    for _ in range(k):
        m = jnp.max(g, axis=-1, keepdims=True)
        hit = (g == m) & (m > NEG)
        idx = jnp.min(jnp.where(hit, lane, LANES), axis=-1, keepdims=True)
        pick = lane == idx
        sel = jnp.where(pick, 1.0, sel)
        g = jnp.where(pick, NEG, g)
    return sel


def _softmax_step(s, allowed, v, m_i, l_i, acc):
    s = jnp.where(allowed, s, NEG)
    m_new = jnp.maximum(m_i, jnp.max(s, axis=-1, keepdims=True))
    m_safe = jnp.where(m_new > NEG, m_new, 0.0)
    alpha = jnp.exp(m_i - m_safe)
    p = jnp.exp(s - m_safe)
    l_new = alpha * l_i + jnp.sum(p, axis=-1, keepdims=True)
    acc_new = alpha * acc + _dot(p.astype(bf16), v)
    return m_new, l_new, acc_new


def _moba_prompt_kernel(slope_ref, q_ref, k_ref, v_ref, o_ref):
    i = pl.program_id(2)
    mb = MOBA_BLOCK
    nb = k_ref.shape[0] // mb
    q = q_ref[...]
    slope = slope_ref[0][:, :1]
    kmean = jnp.concatenate(
        [jnp.mean(k_ref[pl.ds(j * mb, mb), :], axis=0, keepdims=True) for j in range(nb)]
        + [jnp.zeros((LANES - nb, HEAD_DIM), f32)], axis=0)
    gate = _dot_nt(q, kmean, HIGHEST)
    lane = lax.broadcasted_iota(jnp.int32, gate.shape, 1)
    sel = _topk_mask(jnp.where(lane < i, gate, NEG), MOBA_TOPK)
    row = lax.broadcasted_iota(jnp.int32, (mb, mb), 0)
    col = lax.broadcasted_iota(jnp.int32, (mb, mb), 1)
    qb = q.astype(bf16)

    def body(j, carry):
        kj = k_ref[pl.ds(pl.multiple_of(j * mb, mb), mb), :].astype(bf16)
        vj = v_ref[pl.ds(pl.multiple_of(j * mb, mb), mb), :].astype(bf16)
        dist = ((i - j) * mb + row - col).astype(f32)
        s = _dot_nt(qb, kj) * SCALE - slope * dist
        own = (j == i).astype(f32)
        chosen = jnp.max(jnp.where(lane == j, sel, 0.0), axis=-1, keepdims=True) + own
        allowed = (chosen > 0.5) & (col - row <= jnp.where(j < i, mb, 0))
        return _softmax_step(s, allowed, vj, *carry)

    init = (jnp.full((mb, 1), NEG, f32), jnp.zeros((mb, 1), f32), jnp.zeros((mb, HEAD_DIM), f32))
    _, l_i, acc = lax.fori_loop(0, i + 1, body, init)
    o_ref[...] = acc / l_i


def _alibi_table(n):
    s = 2.0 ** (-8.0 * np.arange(1, n + 1) / n)
    return jnp.asarray(np.broadcast_to(s[:, None, None], (n, 1, LANES)), dtype=f32)


def moba_prompt(qm, kvm, batch, seq):
    nq = seq // MOBA_BLOCK
    return pl.pallas_call(
        _moba_prompt_kernel,
        grid=(batch, MOBA_HEADS, nq),
        in_specs=[pl.BlockSpec((1, 1, LANES), lambda b, h, i: (h, 0, 0)),
                  pl.BlockSpec((MOBA_BLOCK, HEAD_DIM), lambda b, h, i: (b * nq + i, h)),
                  pl.BlockSpec((seq, HEAD_DIM), lambda b, h, i: (b, h)),
                  pl.BlockSpec((seq, HEAD_DIM), lambda b, h, i: (b, MOBA_HEADS + h))],
        out_specs=pl.BlockSpec((MOBA_BLOCK, HEAD_DIM), lambda b, h, i: (b * nq + i, h)),
        out_shape=jax.ShapeDtypeStruct((batch * seq, MOBA_W), f32),
        compiler_params=_cparams(("arbitrary", "arbitrary", "arbitrary")),
        name="moba_prompt",
    )(_alibi_table(MOBA_HEADS), qm, kvm, kvm)


CMP_HALF = NSA_CMP_LEN // 2
N_CMP_ROWS = 128


def _compress_rows(xa, xb, w1_ref, w2_ref):
    half = CMP_HALF * HEAD_DIM
    y = _dot(xa, w1_ref[0, :half, :].astype(bf16))
    z = _dot(xb, w1_ref[0, half:, :].astype(bf16))
    parts = []
    for r in range(y.shape[0] // N_CMP_ROWS):
        zr = z[r * N_CMP_ROWS:(r + 1) * N_CMP_ROWS]
        parts.append(y[r * N_CMP_ROWS:(r + 1) * N_CMP_ROWS] + pltpu.roll(zr, N_CMP_ROWS - 1, 0))
    hid = _gelu(jnp.concatenate(parts, axis=0))
    return _dot(hid.astype(bf16), w2_ref[0].astype(bf16))


def _cmp_prompt_kernel(x0_ref, x1_ref, pe_ref, w1_ref, w2_ref, o_ref):
    pe = pe_ref[0]
    xa, xb = [], []
    for x_ref in (x0_ref, x1_ref):
        pa, pb = [], []
        for l in range(CMP_HALF):
            xl = x_ref[pl.ds(l, N_CMP_ROWS, stride=CMP_HALF), :]
            pa.append((xl + pe[l:l + 1]).astype(bf16))
            pb.append((xl + pe[CMP_HALF + l:CMP_HALF + l + 1]).astype(bf16))
        xa.append(jnp.concatenate(pa, axis=1))
        xb.append(jnp.concatenate(pb, axis=1))
    out = _compress_rows(jnp.concatenate(xa, axis=0), jnp.concatenate(xb, axis=0), w1_ref, w2_ref)
    for g in range(NSA_KV_HEADS):
        o_ref[0, 0, g] = out[g * N_CMP_ROWS:(g + 1) * N_CMP_ROWS]


def compress_prompt(kvc, pe2, w12, w22, batch, seq):
    return pl.pallas_call(
        _cmp_prompt_kernel,
        grid=(batch, 2),
        in_specs=[pl.BlockSpec((seq, HEAD_DIM), lambda b, kv: (b, NSA_KV_HEADS * kv)),
                  pl.BlockSpec((seq, HEAD_DIM), lambda b, kv: (b, NSA_KV_HEADS * kv + 1)),
                  pl.BlockSpec((1, NSA_CMP_LEN, HEAD_DIM), lambda b, kv: (kv, 0, 0)),
                  pl.BlockSpec((1, NSA_CMP_LEN * HEAD_DIM, NSA_CMP_HIDDEN), lambda b, kv: (kv, 0, 0)),
                  pl.BlockSpec((1, NSA_CMP_HIDDEN, HEAD_DIM), lambda b, kv: (kv, 0, 0))],
        out_specs=pl.BlockSpec((1, 1, NSA_KV_HEADS, N_CMP_ROWS, HEAD_DIM), lambda b, kv: (b, kv, 0, 0, 0)),
        out_shape=jax.ShapeDtypeStruct((batch, 2, NSA_KV_HEADS, N_CMP_ROWS, HEAD_DIM), f32),
        compiler_params=_cparams(("arbitrary", "arbitrary")),
        name="compress_prompt",
    )(kvc, kvc, pe2, w12, w22)


NSA_TQ = 128
N_CMP = 127


def _masked_softmax(s, mask):
    s = jnp.where(mask, s, NEG)
    m = jnp.max(s, axis=-1, keepdims=True)
    m = jnp.where(m > NEG, m, 0.0)
    p = jnp.where(mask, jnp.exp(s - m), 0.0)
    return p / jnp.maximum(jnp.sum(p, axis=-1, keepdims=True), 1e-30)


def _overlap_matrix():
    c = lax.broadcasted_iota(jnp.int32, (LANES, LANES), 0)
    j = lax.broadcasted_iota(jnp.int32, (LANES, LANES), 1)
    cs = NSA_CMP_STRIDE * c
    bs = NSA_SEL_BLOCK * j
    return ((cs < bs + NSA_SEL_BLOCK) & (cs + NSA_CMP_LEN - 1 >= bs)).astype(f32)


def _nsa_prompt_kernel(slope_ref, q_ref, gt_ref, ck_ref, cv_ref, ks_ref, vs_ref, kw_ref, vw_ref, o_ref):
    g = pl.program_id(1)
    i = pl.program_id(2)
    tq, R, ls = NSA_TQ, NSA_GROUP, NSA_SEL_BLOCK
    rows = R * tq
    q = q_ref[...]
    qs = jnp.concatenate([q[:, r * HEAD_DIM:(r + 1) * HEAD_DIM] for r in range(R)], axis=0)
    qb = qs.astype(bf16)
    slopes = slope_ref[0]
    slope = jnp.concatenate([jnp.broadcast_to(slopes[r:r + 1, :1], (tq, 1)) for r in range(R)], axis=0)
    n1 = lax.broadcasted_iota(jnp.int32, (tq, LANES), 0)
    l1 = lax.broadcasted_iota(jnp.int32, (tq, LANES), 1)
    t1 = i * tq + n1
    nr = lax.broadcasted_iota(jnp.int32, (rows, LANES), 0)
    lr = lax.broadcasted_iota(jnp.int32, (rows, LANES), 1)
    tr = i * tq + (nr & (tq - 1))

    s_c = _dot_nt(qs, ck_ref[0, 0, 0], HIGHEST) * SCALE
    p_c = _masked_softmax(s_c, (NSA_CMP_STRIDE * lr + NSA_CMP_LEN - 1 <= tr) & (lr < N_CMP))
    o_c = _dot(p_c.astype(bf16), cv_ref[0, 0, 0].astype(bf16))

    p_sum = p_c[0:tq]
    for r in range(1, R):
        p_sum = p_sum + p_c[r * tq:(r + 1) * tq]
    imp = _dot(p_sum, _overlap_matrix(), HIGHEST)
    sel = _topk_mask(jnp.where(l1 < t1 // ls, imp, NEG), NSA_SEL_TOPK)
    sel4 = jnp.concatenate([sel] * R, axis=0).astype(bf16)
    jj = lax.broadcasted_iota(jnp.int32, (LANES, tq), 0)
    kk = lax.broadcasted_iota(jnp.int32, (LANES, tq), 1)

    init = (jnp.full((rows, 1), NEG, f32), jnp.zeros((rows, 1), f32), jnp.zeros((rows, HEAD_DIM), f32))

    def slc_body(kt, carry):
        off = pl.multiple_of(kt * tq, tq)
        kj = ks_ref[pl.ds(off, tq), :].astype(bf16)
        vj = vs_ref[pl.ds(off, tq), :].astype(bf16)
        key = kt * tq + lr
        s = _dot_nt(qb, kj) * SCALE - slope * (tr - key).astype(f32)
        expand = (jj == kt * (tq // ls) + kk // ls).astype(bf16)
        chosen = _dot(sel4, expand) > 0.5
        own = (key // ls == tr // ls) & (key <= tr)
        return _softmax_step(s, chosen | own, vj, *carry)

    _, l_s, acc_s = lax.fori_loop(0, i + 1, slc_body, init)
    o_s = acc_s / l_s

    def win_body(kt, carry):
        off = pl.multiple_of(kt * tq, tq)
        kj = kw_ref[pl.ds(off, tq), :].astype(bf16)
        vj = vw_ref[pl.ds(off, tq), :].astype(bf16)
        key = kt * tq + lr
        s = _dot_nt(qb, kj) * SCALE - slope * (tr - key).astype(f32)
        return _softmax_step(s, (key <= tr) & (key > tr - NSA_WINDOW), vj, *carry)

    _, l_w, acc_w = lax.fori_loop(jnp.maximum(i - NSA_WINDOW // tq, 0), i + 1, win_body, init)
    o_w = acc_w / l_w

    gt = gt_ref[...]
    def gate_col(br):
        cols = [jnp.sum(jnp.where(l1 == NSA_BRANCHES * (g * R + r) + br, gt, 0.0), axis=-1, keepdims=True)
                for r in range(R)]
        return jnp.concatenate(cols, axis=0)

    o = gate_col(0) * o_c + gate_col(1) * o_s + gate_col(2) * o_w
    o_ref[...] = jnp.concatenate([o[r * tq:(r + 1) * tq] for r in range(R)], axis=1)


def _alibi_groups(n_groups, group):
    n = n_groups * group
    s = (2.0 ** (-8.0 * np.arange(1, n + 1) / n)).reshape(n_groups, group)
    return jnp.asarray(np.broadcast_to(s[:, :, None], (n_groups, group, LANES)), dtype=f32)


def nsa_prompt(qn, gates, ckv, kvs, kvw, batch, seq):
    nq = seq // NSA_TQ
    G = NSA_KV_HEADS
    kv_spec = lambda off: pl.BlockSpec((seq, HEAD_DIM), lambda b, g, i: (b, off + g))
    return pl.pallas_call(
        _nsa_prompt_kernel,
        grid=(batch, G, nq),
        in_specs=[pl.BlockSpec((1, NSA_GROUP, LANES), lambda b, g, i: (g, 0, 0)),
                  pl.BlockSpec((NSA_TQ, NSA_GROUP * HEAD_DIM), lambda b, g, i: (b * nq + i, g)),
                  pl.BlockSpec((NSA_TQ, LANES), lambda b, g, i: (b * nq + i, 0)),
                  pl.BlockSpec((1, 1, 1, N_CMP_ROWS, HEAD_DIM), lambda b, g, i: (b, 0, g, 0, 0)),
                  pl.BlockSpec((1, 1, 1, N_CMP_ROWS, HEAD_DIM), lambda b, g, i: (b, 1, g, 0, 0)),
                  kv_spec(0), kv_spec(G), kv_spec(0), kv_spec(G)],
        out_specs=pl.BlockSpec((NSA_TQ, NSA_GROUP * HEAD_DIM), lambda b, g, i: (b * nq + i, g)),
        out_shape=jax.ShapeDtypeStruct((batch * seq, NSA_W), f32),
        compiler_params=_cparams(("arbitrary", "arbitrary", "arbitrary")),
        name="nsa_prompt",
    )(_alibi_groups(G, NSA_GROUP), qn, gates, ckv, ckv, kvs, kvs, kvw, kvw)


def _head_slopes(n):
    s = 2.0 ** (-8.0 * np.arange(1, n + 1) / n)
    return jnp.asarray(np.broadcast_to(s[:, None], (n, LANES)), dtype=f32)


def _moba_sample_kernel(pt_ref, slope_ref, q_ref, kvn_ref, *refs):
    page_refs, o_ref = refs[:-1], refs[-1]
    n_pages = len(page_refs)
    mb = MOBA_BLOCK
    ppb = mb // PAGE_SIZE
    nb = n_pages // ppb
    t_new = n_pages * PAGE_SIZE
    q = q_ref[0]
    slope = slope_ref[:, :1]
    tok = lax.broadcasted_iota(jnp.int32, (mb, MOBA_HEADS, 1), 0)
    gates, ms, ls, os_ = [], [], [], []
    for j in range(nb):
        k = jnp.concatenate([page_refs[ppb * j + u][pl.ds(0, PAGE_SIZE, stride=2)] for u in range(ppb)], axis=0)
        v = jnp.concatenate([page_refs[ppb * j + u][pl.ds(1, PAGE_SIZE, stride=2)] for u in range(ppb)], axis=0)
        kmean = jnp.sum(k, axis=0) / mb
        gates.append(jnp.sum(q * kmean, axis=-1, keepdims=True))
        dist = (t_new - j * mb - tok).astype(f32)
        s = jnp.sum(k * q[None], axis=-1, keepdims=True) * SCALE - slope[None] * dist
        m = jnp.max(s, axis=0)
        p = jnp.exp(s - m[None])
        ms.append(m)
        ls.append(jnp.sum(p, axis=0))
        os_.append(jnp.sum(p * v, axis=0))
    chosen = []
    for j in range(nb):
        rank = jnp.zeros_like(gates[j])
        for j2 in range(nb):
            if j2 != j:
                ahead = (gates[j2] >= gates[j]) if j2 < j else (gates[j2] > gates[j])
                rank = rank + ahead.astype(f32)
        chosen.append(rank < MOBA_TOPK)
    kn, vn = kvn_ref[0, 0], kvn_ref[0, 1]
    s_own = jnp.sum(q * kn, axis=-1, keepdims=True) * SCALE
    m_all = s_own
    for j in range(nb):
        m_all = jnp.maximum(m_all, jnp.where(chosen[j], ms[j], NEG))
    w_own = jnp.exp(s_own - m_all)
    l_all = w_own
    o_all = w_own * vn
    for j in range(nb):
        w = jnp.where(chosen[j], jnp.exp(ms[j] - m_all), 0.0)
        l_all = l_all + w * ls[j]
        o_all = o_all + w * os_[j]
    o_ref[0] = o_all / l_all


def moba_sample(q3, kvn4, cache3, page_table):
    nseq, n_pages = page_table.shape
    rows = 2 * PAGE_SIZE
    page_spec = lambda p: pl.BlockSpec((rows, MOBA_HEADS, HEAD_DIM), lambda b, pt: (pt[b, p], 0, 0))
    return pl.pallas_call(
        _moba_sample_kernel,
        grid_spec=pltpu.PrefetchScalarGridSpec(
            num_scalar_prefetch=1, grid=(nseq,),
            in_specs=[pl.BlockSpec((MOBA_HEADS, LANES), lambda b, pt: (0, 0)),
                      pl.BlockSpec((1, MOBA_HEADS, HEAD_DIM), lambda b, pt: (b, 0, 0)),
                      pl.BlockSpec((1, 2, MOBA_HEADS, HEAD_DIM), lambda b, pt: (b, 0, 0, 0))]
                     + [page_spec(p) for p in range(n_pages)],
            out_specs=pl.BlockSpec((1, MOBA_HEADS, HEAD_DIM), lambda b, pt: (b, 0, 0))),
        out_shape=jax.ShapeDtypeStruct((nseq, MOBA_HEADS, HEAD_DIM), f32),
        compiler_params=_cparams(("arbitrary",)),
        name="moba_sample",
    )(page_table, _head_slopes(MOBA_HEADS), q3, kvn4, *([cache3] * n_pages))


KV_ROWS = 2 * NSA_KV_HEADS


def _nsa_sample_cmp_kernel(pt_ref, q_ref, pe_ref, w1k_ref, w1v_ref, w2k_ref, w2v_ref, *refs):
    page_refs, (oc_ref, sel_ref) = refs[:-2], refs[-2:]
    n_pages = len(page_refs)
    per_page = PAGE_SIZE // CMP_HALF
    t_new = n_pages * PAGE_SIZE
    G, R = NSA_KV_HEADS, NSA_GROUP
    q = q_ref[0]
    row = lax.broadcasted_iota(jnp.int32, (G * R, LANES), 0)
    lane = lax.broadcasted_iota(jnp.int32, (G * R, LANES), 1)
    comp = []
    for kv, (w1_ref, w2_ref) in enumerate(((w1k_ref, w2k_ref), (w1v_ref, w2v_ref))):
        pe = pe_ref[kv]
        xa, xb = [], []
        for g in range(G):
            pa, pb = [], []
            for l in range(CMP_HALF):
                xl = jnp.concatenate(
                    [pr[pl.ds(KV_ROWS * l + G * kv + g, per_page, stride=KV_ROWS * CMP_HALF), :] for pr in page_refs], axis=0)
                pa.append((xl + pe[l:l + 1]).astype(bf16))
                pb.append((xl + pe[CMP_HALF + l:CMP_HALF + l + 1]).astype(bf16))
            xa.append(jnp.concatenate(pa, axis=1))
            xb.append(jnp.concatenate(pb, axis=1))
        comp.append(_compress_rows(jnp.concatenate(xa, axis=0), jnp.concatenate(xb, axis=0), w1_ref, w2_ref))
    ck, cv = comp
    n_cmp = (t_new + 1 - NSA_CMP_LEN) // NSA_CMP_STRIDE + 1
    visible = (lane < n_cmp) & (NSA_CMP_STRIDE * lane + NSA_CMP_LEN - 1 <= t_new)
    o_c = jnp.zeros((G * R, HEAD_DIM), f32)
    imp_rows = jnp.full((G * R, LANES), NEG, f32)
    overlap = _overlap_matrix()
    for g in range(G):
        ck_g = ck[g * N_CMP_ROWS:(g + 1) * N_CMP_ROWS]
        cv_g = cv[g * N_CMP_ROWS:(g + 1) * N_CMP_ROWS]
        mine = (row >= g * R) & (row < (g + 1) * R)
        p_c = _masked_softmax(_dot_nt(q, ck_g, HIGHEST) * SCALE, visible)
        o_c = jnp.where(mine, _dot(p_c.astype(bf16), cv_g.astype(bf16)), o_c)
        p_sum = jnp.sum(jnp.where(mine, p_c, 0.0), axis=0, keepdims=True)
        imp = _dot(jnp.broadcast_to(p_sum, (G * R, LANES)), overlap, HIGHEST)
        imp_rows = jnp.where((row == g) & (lane < t_new // NSA_SEL_BLOCK), imp, imp_rows)
    oc_ref[0] = o_c
    sel = _topk_mask(imp_rows, NSA_SEL_TOPK)
    out = jnp.zeros((G * R, LANES), jnp.int32)
    remaining = sel
    for s in range(NSA_SEL_TOPK):
        idx = jnp.min(jnp.where(remaining > 0.5, lane, LANES), axis=-1, keepdims=True)
        found = idx < LANES
        out = jnp.where(lane == s, jnp.where(found, idx, 0), out)
        out = jnp.where(lane == NSA_SEL_TOPK + s, found.astype(jnp.int32), out)
        remaining = jnp.where(lane == idx, 0.0, remaining)
    sel_ref[0] = out


def nsa_sample_cmp(q3, cache2, page_table, pe2, w1k, w1v, w2k, w2v):
    nseq, n_pages = page_table.shape
    rows = PAGE_SIZE * KV_ROWS
    page_spec = lambda p: pl.BlockSpec((rows, HEAD_DIM), lambda b, pt: (pt[b, p], 0))
    full = lambda a: pl.BlockSpec(a.shape, lambda b, pt: (0,) * a.ndim)
    return pl.pallas_call(
        _nsa_sample_cmp_kernel,
        grid_spec=pltpu.PrefetchScalarGridSpec(
            num_scalar_prefetch=1, grid=(nseq,),
            in_specs=[pl.BlockSpec((1, NSA_HEADS, HEAD_DIM), lambda b, pt: (b, 0, 0)),
                      full(pe2), full(w1k), full(w1v), full(w2k), full(w2v)]
                     + [page_spec(p) for p in range(n_pages)],
            out_specs=[pl.BlockSpec((1, NSA_HEADS, HEAD_DIM), lambda b, pt: (b, 0, 0)),
                       pl.BlockSpec((1, NSA_HEADS, LANES), lambda b, pt: (b, 0, 0))]),
        out_shape=[jax.ShapeDtypeStruct((nseq, NSA_HEADS, HEAD_DIM), f32),
                   jax.ShapeDtypeStruct((nseq, NSA_HEADS, LANES), jnp.int32)],
        compiler_params=_cparams(("arbitrary",)),
        name="nsa_sample_cmp",
    )(page_table, q3, pe2, w1k, w1v, w2k, w2v, *([cache2] * n_pages))


def _decode_attend(q, slope, keys, vals, pos, valid, k_own, v_own, t_new):
    s = _dot_nt(q.astype(bf16), keys.astype(bf16)) * SCALE - slope * (t_new - pos).astype(f32)
    s = jnp.where(valid, s, NEG)
    s_own = jnp.sum(q * k_own, axis=-1, keepdims=True) * SCALE
    m = jnp.maximum(jnp.max(s, axis=-1, keepdims=True), s_own)
    p = jnp.exp(s - m)
    p_own = jnp.exp(s_own - m)
    denom = jnp.sum(p, axis=-1, keepdims=True) + p_own
    return (_dot(p.astype(bf16), vals.astype(bf16)) + p_own * v_own) / denom


def _nsa_sample_attn_kernel(pt_ref, sel_ref, slope_ref, q_ref, gt_ref, oc_ref, ksn_ref, kwn_ref, win_ref, *refs):
    blk_refs, (o_ref, wout_ref) = refs[:-2], refs[-2:]
    b = pl.program_id(0)
    G, R, ls, K = NSA_KV_HEADS, NSA_GROUP, NSA_SEL_BLOCK, NSA_SEL_TOPK
    t_new = pt_ref.shape[1] * PAGE_SIZE
    q = q_ref[0]
    slope = slope_ref[:, :1]
    row = lax.broadcasted_iota(jnp.int32, (G * R, HEAD_DIM), 0)
    own_rows = lambda ref, kv: jnp.where(row < R, ref[0, G * kv:G * kv + 1], ref[0, G * kv + 1:G * kv + 2])
    lane_s = lax.broadcasted_iota(jnp.int32, (1, K * ls), 1)
    n_win = win_ref.shape[0] // KV_ROWS
    lane_w = lax.broadcasted_iota(jnp.int32, (1, n_win), 1)
    pos_w = t_new - n_win + lane_w
    o_s = jnp.zeros((G * R, HEAD_DIM), f32)
    o_w = jnp.zeros((G * R, HEAD_DIM), f32)
    for g in range(G):
        keys = jnp.concatenate([blk_refs[g * K + s][pl.ds(g, ls, stride=KV_ROWS), :] for s in range(K)], axis=0)
        vals = jnp.concatenate([blk_refs[g * K + s][pl.ds(G + g, ls, stride=KV_ROWS), :] for s in range(K)], axis=0)
        pos = jnp.zeros((1, K * ls), jnp.int32)
        valid = jnp.zeros((1, K * ls), jnp.int32)
        for s in range(K):
            here = lane_s // ls == s
            pos = jnp.where(here, sel_ref[b, g * 2 * K + s] * ls + lane_s - s * ls, pos)
            valid = jnp.where(here, sel_ref[b, g * 2 * K + K + s], valid)
        mine = (row >= g * R) & (row < (g + 1) * R)
        o_s = jnp.where(mine, _decode_attend(q, slope, keys, vals, pos, valid > 0, own_rows(ksn_ref, 0), own_rows(ksn_ref, 1), t_new), o_s)
        keys_w = win_ref[pl.ds(g, n_win, stride=KV_ROWS), :]
        vals_w = win_ref[pl.ds(G + g, n_win, stride=KV_ROWS), :]
        ok_w = (pos_w > t_new - NSA_WINDOW) & (pos_w >= 0)
        o_w = jnp.where(mine, _decode_attend(q, slope, keys_w, vals_w, pos_w, ok_w, own_rows(kwn_ref, 0), own_rows(kwn_ref, 1), t_new), o_w)
    gt = gt_ref[0]
    o_ref[0] = gt[:, 0:1] * oc_ref[0] + gt[:, 1:2] * o_s + gt[:, 2:3] * o_w
    total = win_ref.shape[0]
    shifted = pltpu.roll(win_ref[...], total - KV_ROWS, 0)
    new8 = jnp.concatenate([kwn_ref[0], kwn_ref[0]], axis=0)
    row8 = lax.broadcasted_iota(jnp.int32, (2 * KV_ROWS, HEAD_DIM), 0)
    wout_ref[pl.ds(0, total - 2 * KV_ROWS), :] = shifted[:total - 2 * KV_ROWS]
    wout_ref[pl.ds(total - 2 * KV_ROWS, 2 * KV_ROWS), :] = jnp.where(row8 >= KV_ROWS, new8, shifted[total - 2 * KV_ROWS:])


def nsa_sample_attn(q3, gates3, o_c, ksn, kwn, slc2, win2, page_table, sel_flat):
    nseq = page_table.shape[0]
    G, K, ls = NSA_KV_HEADS, NSA_SEL_TOPK, NSA_SEL_BLOCK
    blocks_per_page = PAGE_SIZE // ls
    n_blocks = page_table.shape[1] * blocks_per_page
    win_rows = win2.shape[0] // nseq

    def blk_spec(g, s):
        def index(b, pt, sel):
            bb = jnp.minimum(b, nseq - 1)
            blk = jnp.clip(sel[bb, g * 2 * K + s], 0, n_blocks - 1)
            return (pt[bb, blk // blocks_per_page] * blocks_per_page + blk % blocks_per_page, 0)
        return pl.BlockSpec((ls * KV_ROWS, HEAD_DIM), index)

    per_seq = lambda shape: pl.BlockSpec((1,) + shape, lambda b, pt, sel: (b,) + (0,) * len(shape))
    return pl.pallas_call(
        _nsa_sample_attn_kernel,
        grid_spec=pltpu.PrefetchScalarGridSpec(
            num_scalar_prefetch=2, grid=(nseq,),
            in_specs=[pl.BlockSpec((NSA_HEADS, LANES), lambda b, pt, sel: (0, 0)),
                      per_seq((NSA_HEADS, HEAD_DIM)), per_seq((NSA_HEADS, LANES)), per_seq((NSA_HEADS, HEAD_DIM)),
                      per_seq((KV_ROWS, HEAD_DIM)), per_seq((KV_ROWS, HEAD_DIM)),
                      pl.BlockSpec((win_rows, HEAD_DIM), lambda b, pt, sel: (b, 0))]
                     + [blk_spec(g, s) for g in range(G) for s in range(K)],
            out_specs=[per_seq((NSA_HEADS, HEAD_DIM)),
                       pl.BlockSpec((win_rows, HEAD_DIM), lambda b, pt, sel: (b, 0))]),
        out_shape=[jax.ShapeDtypeStruct((nseq, NSA_HEADS, HEAD_DIM), f32),
                   jax.ShapeDtypeStruct(win2.shape, f32)],
        compiler_params=_cparams(("arbitrary",)),
        name="nsa_sample_attn",
    )(page_table, sel_flat, _head_slopes(NSA_HEADS), q3, gates3, o_c, ksn, kwn, win2, *([slc2] * (G * K)))


def _mid_kernel(x_ref, om_ref, on_ref, w_ref, g_ref, gate_ref, sh_ref, sc_ref, x1_ref, h2_ref):
    proj = (_dot(om_ref[...].astype(bf16), w_ref[:MOBA_W, :]) + _dot(on_ref[...].astype(bf16), w_ref[MOBA_W:, :]))
    x1 = x_ref[...] + gate_ref[0] * proj
    x1_ref[...] = x1
    h2_ref[...] = _rms_mod(x1, g_ref[...], sh_ref[0], sc_ref[0]).astype(bf16)


def mid_block(x, o_m, o_n, w_out_b, g_norm2, mods3, tm, rows_per_mod):
    t = x.shape[0]
    r = mods3.shape[1]
    tiles_per_mod = rows_per_mod // tm
    mod_spec = lambda which: pl.BlockSpec((1, r, D_MODEL), lambda i: (i // tiles_per_mod, 0, which))
    return pl.pallas_call(
        _mid_kernel,
        grid=(t // tm,),
        in_specs=[pl.BlockSpec((tm, D_MODEL), lambda i: (i, 0)),
                  pl.BlockSpec((tm, MOBA_W), lambda i: (i, 0)),
                  pl.BlockSpec((tm, NSA_W), lambda i: (i, 0)),
                  pl.BlockSpec((MOBA_W + NSA_W, D_MODEL), lambda i: (0, 0)),
                  pl.BlockSpec((1, D_MODEL), lambda i: (0, 0)),
                  mod_spec(2), mod_spec(3), mod_spec(4)],
        out_specs=[pl.BlockSpec((tm, D_MODEL), lambda i: (i, 0)),
                   pl.BlockSpec((tm, D_MODEL), lambda i: (i, 0))],
        out_shape=[jax.ShapeDtypeStruct((t, D_MODEL), f32), jax.ShapeDtypeStruct((t, D_MODEL), bf16)],
        compiler_params=_cparams(("arbitrary",)),
        name="mid_block",
    )(x, o_m, o_n, w_out_b, g_norm2.reshape(1, D_MODEL), mods3, mods3, mods3)


PEER_HALF = PEER_QDIM // 2
PEER_A_FULL = 8


def _top_values(s, k):
    tops = []
    cur = s
    for _ in range(k):
        m = jnp.max(cur, axis=0, keepdims=True)
        tops.append(m)
        cur = jnp.where(cur == m, NEG, cur)
    return jnp.concatenate(tops, axis=0)


def _peer_route_kernel(h_ref, wq_ref, keys_ref, s1_ref, thr_ref, e1_ref, coef_ref):
    q = _dot(h_ref[...], wq_ref[...])
    k = PEER_TOPK
    for h in range(PEER_HEADS):
        base = h * PEER_QDIM
        s0 = _dot_nt(keys_ref[h, 0], q[:, base:base + PEER_HALF], HIGHEST)
        s1 = _dot_nt(keys_ref[h, 1], q[:, base + PEER_HALF:base + PEER_QDIM], HIGHEST)
        top0 = _top_values(s0, k)
        top1 = _top_values(s1, k)
        cand = jnp.concatenate([top0[a:a + 1] + top1 for a in range(PEER_A_FULL)]
                               + [top0[PEER_A_FULL:] + top1[0:1]], axis=0)
        best = _top_values(cand, k)
        tau = best[k - 1:k]
        z = jnp.sum(jnp.exp(best - best[0:1]), axis=0, keepdims=True)
        thr = jnp.full(s0.shape, jnp.inf, f32)
        for a in range(k):
            thr_a = jnp.min(jnp.where(top0[a:a + 1] + top1 >= tau, top1, jnp.inf), axis=0, keepdims=True)
            thr = jnp.where(s0 == top0[a:a + 1], thr_a, thr)
        s1_ref[h] = s1
        thr_ref[h] = thr
        e1_ref[h] = jnp.exp(s1 - top1[0:1])
        coef_ref[h] = jnp.exp(s0 - top0[0:1]) / z


def peer_route(h2, wq_b, keys, tm):
    t = h2.shape[0]
    out_spec = pl.BlockSpec((PEER_HEADS, PEER_KEYS, tm), lambda i: (0, 0, i))
    out_shape = jax.ShapeDtypeStruct((PEER_HEADS, PEER_KEYS, t), f32)
    return pl.pallas_call(
        _peer_route_kernel,
        grid=(t // tm,),
        in_specs=[pl.BlockSpec((tm, D_MODEL), lambda i: (i, 0)),
                  pl.BlockSpec(wq_b.shape, lambda i: (0, 0)),
                  pl.BlockSpec(keys.shape, lambda i: (0, 0, 0, 0))],
        out_specs=[out_spec] * 4,
        out_shape=[out_shape] * 4,
        compiler_params=_cparams(("arbitrary",)),
        name="peer_route",
    )(h2, wq_b, keys)


def _peer_expert_kernel(h_ref, u_ref, vt_ref, s1_ref, thr_ref, e1_ref, coef_ref, x1_ref, gate_ref, gf_ref,
                        y_ref, acc_ref):
    e = pl.program_id(1)
    te = u_ref.shape[0]
    rows_per_step = te // PEER_KEYS

    @pl.when(e == 0)
    def _():
        acc_ref[...] = jnp.zeros_like(acc_ref)

    act = _gelu(_dot_nt(u_ref[...], h_ref[...]))
    parts = []
    for r in range(rows_per_step):
        i0 = e * rows_per_step + r
        w = jnp.zeros((PEER_KEYS, act.shape[1]), f32)
        for h in range(PEER_HEADS):
            thr = thr_ref[h, pl.ds(i0, 1), :]
            coef = coef_ref[h, pl.ds(i0, 1), :]
            w = w + jnp.where(s1_ref[h] >= thr, e1_ref[h] * coef, 0.0)
        parts.append((w * act[r * PEER_KEYS:(r + 1) * PEER_KEYS]).astype(bf16))
    acc_ref[...] += _dot(vt_ref[...], jnp.concatenate(parts, axis=0))

    @pl.when(e == pl.num_programs(1) - 1)
    def _():
        y = x1_ref[...] + gate_ref[0] * acc_ref[...].T
        y_ref[...] = (y * lax.rsqrt(jnp.mean(y * y, axis=-1, keepdims=True) + RMS_EPS)) * gf_ref[...]


def peer_experts(h2, u_b, vt_b, route, x1, mods3, g_final, tm, te, rows_per_mod):
    t = h2.shape[0]
    r = mods3.shape[1]
    n_exp = u_b.shape[0]
    tiles_per_mod = rows_per_mod // tm
    route_spec = pl.BlockSpec((PEER_HEADS, PEER_KEYS, tm), lambda i, e: (0, 0, i))
    return pl.pallas_call(
        _peer_expert_kernel,
        grid=(t // tm, n_exp // te),
        in_specs=[pl.BlockSpec((tm, D_MODEL), lambda i, e: (i, 0)),
                  pl.BlockSpec((te, D_MODEL), lambda i, e: (e, 0)),
                  pl.BlockSpec((D_MODEL, te), lambda i, e: (0, e)),
                  route_spec, route_spec, route_spec, route_spec,
                  pl.BlockSpec((tm, D_MODEL), lambda i, e: (i, 0)),
                  pl.BlockSpec((1, r, D_MODEL), lambda i, e: (i // tiles_per_mod, 0, 5)),
                  pl.BlockSpec((1, D_MODEL), lambda i, e: (0, 0))],
        out_specs=pl.BlockSpec((tm, D_MODEL), lambda i, e: (i, 0)),
        out_shape=jax.ShapeDtypeStruct((t, D_MODEL), f32),
        scratch_shapes=[pltpu.VMEM((D_MODEL, tm), f32)],
        compiler_params=_cparams(("arbitrary", "arbitrary")),
        name="peer_experts",
    )(h2, u_b, vt_b, *route, x1, mods3, g_final.reshape(1, D_MODEL))


def _group_forward(x2, mods3, tm, rows_per_mod, w, attend):
    proj = in_projection(x2, mods3, w["g_norm1"], w["w_in"], tm, rows_per_mod)
    o_m, o_n = attend(proj)
    x1, h2 = mid_block(x2, o_m, o_n, w["w_out"], w["g_norm2"], mods3, min(tm, 256), rows_per_mod)
    route = peer_route(h2, w["peer_w_q"], w["peer_keys"], min(tm, 256))
    y = peer_experts(h2, w["peer_u"], w["peer_vt"], route, x1, mods3, w["g_final"], tm, 512, rows_per_mod)
    return proj, y


def kernel(x_prompt, x_sample, cache_moba_kv, cache_nsa_cmp_kv, cache_nsa_slc_kv, state_nsa_win_kv, page_table,
           c_prompt, c_sample, w_ada, b_ada, g_norm1, w_in, cmp_pe_k, cmp_w1_k, cmp_w2_k, cmp_pe_v, cmp_w1_v,
           cmp_w2_v, w_out, g_norm2, peer_w_q, peer_keys, peer_u, peer_v, g_final):
    assert w_ada.shape[0] == 1, "single layer"
    batch, seq, _ = x_prompt.shape
    nseq, dec_seq, _ = x_sample.shape
    assert dec_seq == 1 and state_nsa_win_kv.shape[2] == NSA_WINDOW and seq >= NSA_WINDOW
    G, H, dh = NSA_KV_HEADS, MOBA_HEADS, HEAD_DIM

    c_all = jnp.concatenate([c_prompt, c_sample], axis=0)
    pad = (-c_all.shape[0]) % 8
    mods = ada_mods(jnp.pad(c_all, ((0, pad), (0, 0))), w_ada[0], b_ada[0])
    mods_p = mods[:batch].reshape(batch, 1, N_MOD * D_MODEL)
    mods_s = mods[batch:batch + nseq].reshape(1, nseq, N_MOD * D_MODEL)

    p_in = w_in.shape[2]
    pe2 = jnp.stack([cmp_pe_k[0], cmp_pe_v[0]])
    w12 = jnp.stack([cmp_w1_k[0], cmp_w1_v[0]]).astype(bf16)
    w22 = jnp.stack([cmp_w2_k[0], cmp_w2_v[0]]).astype(bf16)
    w = dict(
        g_norm1=g_norm1[0], g_norm2=g_norm2[0], g_final=g_final,
        w_in=jnp.pad(w_in[0], ((0, 0), (0, IN_COLS - p_in))).astype(bf16),
        w_out=w_out[0].astype(bf16),
        peer_w_q=peer_w_q[0].astype(bf16), peer_keys=peer_keys[0],
        peer_u=peer_u[0].astype(bf16), peer_vt=peer_v[0].T.astype(bf16),
    )

    def attend_prompt(proj):
        qm, kvm, qn, kvc, kvs, kvw, gates = proj
        ckv = compress_prompt(kvc, pe2, w12, w22, batch, seq)
        return moba_prompt(qm, kvm, batch, seq), nsa_prompt(qn, gates, ckv, kvs, kvw, batch, seq)

    win_out = []

    def attend_sample(proj):
        qm, kvm, qn, kvc, kvs, kvw, gates = proj
        o_m = moba_sample(qm.reshape(nseq, H, dh), kvm.reshape(nseq, 2, H, dh), cache_moba_kv.reshape(-1, H, dh), page_table)
        qn3 = qn.reshape(nseq, NSA_HEADS, dh)
        o_c, sel = nsa_sample_cmp(qn3, cache_nsa_cmp_kv.reshape(-1, dh), page_table, pe2,
                                  w12[0:1], w12[1:2], w22[0:1], w22[1:2])
        sel_flat = sel[:, :G, :2 * NSA_SEL_TOPK].reshape(nseq, G * 2 * NSA_SEL_TOPK)
        gates3 = jnp.pad(gates[:, :NSA_HEADS * NSA_BRANCHES].reshape(nseq, NSA_HEADS, NSA_BRANCHES),
                         ((0, 0), (0, 0), (0, LANES - NSA_BRANCHES)))
        o_n, wout = nsa_sample_attn(qn3, gates3, o_c, kvs.reshape(nseq, KV_ROWS, dh), kvw.reshape(nseq, KV_ROWS, dh),
                                    cache_nsa_slc_kv.reshape(-1, dh), state_nsa_win_kv.reshape(-1, dh), page_table, sel_flat)
        win_out.append(wout)
        return o_m.reshape(nseq, MOBA_W), o_n.reshape(nseq, NSA_W)

    proj_p, y_p = _group_forward(x_prompt.reshape(batch * seq, D_MODEL), mods_p, 512, seq, w, attend_prompt)
    proj_s, y_s = _group_forward(x_sample.reshape(nseq, D_MODEL), mods_s, nseq, nseq, w, attend_sample)

    kv_p = lambda a, nh: a.reshape(1, batch, seq, 2, nh, dh)
    kv_s = lambda a, nh: a.reshape(1, nseq, 1, 2, nh, dh)
    return (y_p.reshape(batch, seq, D_MODEL), y_s.reshape(nseq, 1, D_MODEL),
            kv_p(proj_p[1], H), kv_s(proj_s[1], H),
            kv_p(proj_p[3], G), kv_s(proj_s[3], G),
            kv_p(proj_p[4], G), kv_s(proj_s[4], G),
            kv_p(proj_p[5], G)[:, :, seq - NSA_WINDOW:],
            win_out[0].reshape(state_nsa_win_kv.shape))
```

```python
import functools

import jax
import jax.numpy as jnp
import numpy as np
from jax import lax
from jax.experimental import pallas as pl
from jax.experimental.pallas import tpu as pltpu

f32 = jnp.float32
bf16 = jnp.bfloat16
HIGHEST = lax.Precision.HIGHEST

D_MODEL = 2048
HEAD_DIM = 128
MOBA_HEADS = 8
NSA_HEADS = 8
NSA_KV_HEADS = 2
NSA_GROUP = 4
MOBA_W = MOBA_HEADS * HEAD_DIM
NSA_W = NSA_HEADS * HEAD_DIM
NSA_KVW = NSA_KV_HEADS * HEAD_DIM
MOBA_BLOCK = 256
MOBA_TOPK = 3
NSA_CMP_LEN = 32
NSA_CMP_STRIDE = 16
NSA_CMP_HIDDEN = 256
NSA_SEL_BLOCK = 64
NSA_SEL_TOPK = 4
NSA_WINDOW = 512
NSA_BRANCHES = 3
PEER_KEYS = 128
PEER_HEADS = 8
PEER_QDIM = 256
PEER_TOPK = 16
N_MOD = 6
RMS_EPS = 1e-6
PAGE_SIZE = 128
SCALE = HEAD_DIM ** -0.5
NEG = -jnp.inf
LANES = 128

IN_TILE = 512
IN_COLS = 12 * IN_TILE
VMEM_LIMIT = 56 * 1024 * 1024


def _cparams(sem):
    return pltpu.CompilerParams(dimension_semantics=sem, vmem_limit_bytes=VMEM_LIMIT)


def _gelu(x):
    z2 = np.float32(2.0 * np.sqrt(2.0 / np.pi)) * (x + 0.044715 * (x * x * x))
    return x / (1.0 + jnp.exp(-z2))


def _dot_nt(a, b, precision=None):
    return lax.dot_general(a, b, (((1,), (1,)), ((), ())), precision=precision, preferred_element_type=f32)


def _dot(a, b, precision=None):
    return jnp.dot(a, b, precision=precision, preferred_element_type=f32)


def _ada_kernel(c_ref, w_ref, b_ref, o_ref):
    c = c_ref[...]
    a = c * jax.nn.sigmoid(c)
    o_ref[...] = _dot(a, w_ref[...], HIGHEST) + b_ref[...]


def ada_mods(c, w_ada, b_ada):
    rows = c.shape[0]
    n = w_ada.shape[1]
    tn = 1024
    return pl.pallas_call(
        _ada_kernel,
        grid=(n // tn,),
        in_specs=[pl.BlockSpec((rows, D_MODEL), lambda j: (0, 0)),
                  pl.BlockSpec((D_MODEL, tn), lambda j: (0, j)),
                  pl.BlockSpec((1, tn), lambda j: (0, j))],
        out_specs=pl.BlockSpec((rows, tn), lambda j: (0, j)),
        out_shape=jax.ShapeDtypeStruct((rows, n), f32),
        compiler_params=_cparams(("arbitrary",)),
        name="ada_mods",
    )(c, w_ada, b_ada.reshape(1, n))


def _rms_mod(x, g, shift, scale):
    y = x * lax.rsqrt(jnp.mean(x * x, axis=-1, keepdims=True) + RMS_EPS)
    return (y * g) * (1.0 + scale) + shift


def _inproj_kernel(x_ref, g_ref, sh_ref, sc_ref, w_ref,
                   qm_ref, kvm_ref, qn_ref, kvc_ref, kvs_ref, kvw_ref, gt_ref, h_scr):
    j = pl.program_id(1)

    @pl.when(j == 0)
    def _():
        h_scr[...] = _rms_mod(x_ref[...], g_ref[...], sh_ref[0], sc_ref[0]).astype(bf16)

    acc = _dot(h_scr[...], w_ref[...])

    @pl.when(j < 2)
    def _():
        qm_ref[...] = acc

    @pl.when((j >= 2) & (j < 6))
    def _():
        kvm_ref[...] = acc

    @pl.when((j >= 6) & (j < 8))
    def _():
        qn_ref[...] = acc

    @pl.when(j == 8)
    def _():
        kvc_ref[...] = acc

    @pl.when(j == 9)
    def _():
        kvs_ref[...] = acc

    @pl.when(j == 10)
    def _():
        kvw_ref[...] = acc

    @pl.when(j == 11)
    def _():
        gt_ref[...] = jax.nn.sigmoid(acc[:, :LANES])


def in_projection(x, mods3, g_norm1, w_in_p, tm, rows_per_mod):
    t = x.shape[0]
    r = mods3.shape[1]
    tiles_per_mod = rows_per_mod // tm
    mod_spec = lambda which: pl.BlockSpec((1, r, D_MODEL), lambda i, j: (i // tiles_per_mod, 0, which))
    clip = lambda j, lo, n: jnp.clip(j - lo, 0, n - 1)
    out_shapes = [jax.ShapeDtypeStruct((t, w), f32) for w in (MOBA_W, 2 * MOBA_W, NSA_W, 2 * NSA_KVW, 2 * NSA_KVW, 2 * NSA_KVW, LANES)]
    out_specs = [
        pl.BlockSpec((tm, IN_TILE), lambda i, j: (i, clip(j, 0, 2))),
        pl.BlockSpec((tm, IN_TILE), lambda i, j: (i, clip(j, 2, 4))),
        pl.BlockSpec((tm, IN_TILE), lambda i, j: (i, clip(j, 6, 2))),
        pl.BlockSpec((tm, IN_TILE), lambda i, j: (i, 0)),
        pl.BlockSpec((tm, IN_TILE), lambda i, j: (i, 0)),
        pl.BlockSpec((tm, IN_TILE), lambda i, j: (i, 0)),
        pl.BlockSpec((tm, LANES), lambda i, j: (i, 0)),
    ]
    return pl.pallas_call(
        _inproj_kernel,
        grid=(t // tm, IN_COLS // IN_TILE),
        in_specs=[pl.BlockSpec((tm, D_MODEL), lambda i, j: (i, 0)),
                  pl.BlockSpec((1, D_MODEL), lambda i, j: (0, 0)),
                  mod_spec(0), mod_spec(1),
                  pl.BlockSpec((D_MODEL, IN_TILE), lambda i, j: (0, j))],
        out_specs=out_specs,
        out_shape=out_shapes,
        scratch_shapes=[pltpu.VMEM((tm, D_MODEL), bf16)],
        compiler_params=_cparams(("arbitrary", "arbitrary")),
        name="in_projection",
    )(x, g_norm1.reshape(1, D_MODEL), mods3, mods3, w_in_p)


def _topk_mask(score, k):
    lane = lax.broadcasted_iota(jnp.int32, score.shape, 1)
    sel = jnp.zeros(score.shape, f32)
    g = score
    for _ in range(k):
        m = jnp.max(g, axis=-1, keepdims=True)
        hit = (g == m) & (m > NEG)
        idx = jnp.min(jnp.where(hit, lane, LANES), axis=-1, keepdims=True)
        pick = lane == idx
        sel = jnp.where(pick, 1.0, sel)
        g = jnp.where(pick, NEG, g)
    return sel


def _topk_rows(score, k):
    rows = score.shape[0]
    sub = lax.broadcasted_iota(jnp.int32, score.shape, 0)
    sel = jnp.zeros(score.shape, f32)
    g = score
    for _ in range(k):
        m = jnp.max(g, axis=0, keepdims=True)
        hit = (g == m) & (m > NEG)
        idx = jnp.min(jnp.where(hit, sub, rows), axis=0, keepdims=True)
        pick = sub == idx
        sel = jnp.where(pick, 1.0, sel)
        g = jnp.where(pick, NEG, g)
    return sel


def _softmax_first(s, vt):
    m = jnp.max(s, axis=0, keepdims=True)
    p = jnp.exp(s - m)
    return m, jnp.sum(p, axis=0, keepdims=True), _dot(vt, p.astype(bf16))


def _softmax_next(s, vt, m_i, l_i, acc):
    m_new = jnp.maximum(m_i, jnp.max(s, axis=0, keepdims=True))
    alpha = jnp.exp(m_i - m_new)
    p = jnp.exp(s - m_new)
    return m_new, alpha * l_i + jnp.sum(p, axis=0, keepdims=True), alpha * acc + _dot(vt, p.astype(bf16))


MOBA_HEADS_PER_STEP = 4


def _moba_prompt_kernel(slope_ref, q_ref, k_ref, vt_ref, o_ref, sel_ref):
    i = pl.program_id(2)
    mb, dh, hg = MOBA_BLOCK, HEAD_DIM, MOBA_HEADS_PER_STEP
    nb = k_ref.shape[0] // mb
    krow = lax.broadcasted_iota(jnp.int32, (mb, mb), 0)
    qcol = lax.broadcasted_iota(jnp.int32, (mb, mb), 1)
    rel = (qcol - krow).astype(f32)
    blk = lax.broadcasted_iota(jnp.int32, (nb, mb), 0)
    own = pl.multiple_of(i * mb, mb)
    heads = []
    for hh in range(hg):
        cs = slice(hh * dh, (hh + 1) * dh)
        q = q_ref[:, cs]
        slope = slope_ref[0, hh:hh + 1, :1]
        kmean = jnp.concatenate(
            [jnp.mean(k_ref[pl.ds(j * mb, mb), cs], axis=0, keepdims=True) for j in range(nb)], axis=0)
        gate = _dot_nt(kmean, q, HIGHEST)
        sel_ref[hh] = _topk_rows(jnp.where(blk < i, gate, NEG), MOBA_TOPK)
        qb = (q * SCALE).astype(bf16)
        bias = slope * rel
        s = _dot_nt(k_ref[pl.ds(own, mb), cs].astype(bf16), qb) - bias
        s = jnp.where(krow <= qcol, s, NEG)
        heads.append((cs, qb, bias, slope, _softmax_first(s, vt_ref[cs, pl.ds(own, mb)].astype(bf16))))

    def body(j, carry):
        off = pl.multiple_of(j * mb, mb)
        out = []
        for hh, (cs, qb, bias, slope, _) in enumerate(heads):
            s = _dot_nt(k_ref[pl.ds(off, mb), cs].astype(bf16), qb) - bias - slope * ((i - j) * mb).astype(f32)
            s = jnp.where(sel_ref[hh, pl.ds(j, 1), :] > 0.5, s, NEG)
            out.append(_softmax_next(s, vt_ref[cs, pl.ds(off, mb)].astype(bf16), *carry[hh]))
        return tuple(out)

    final = lax.fori_loop(0, i, body, tuple(h[4] for h in heads))
    for hh, (cs, *_rest) in enumerate(heads):
        _, l_i, acc = final[hh]
        o_ref[:, cs] = (acc / l_i).T


def _alibi_table(n):
    s = 2.0 ** (-8.0 * np.arange(1, n + 1) / n)
    return jnp.asarray(np.broadcast_to(s[:, None, None], (n, 1, LANES)), dtype=f32)


def moba_prompt(qm, kvm, vmt, batch, seq):
    nq = seq // MOBA_BLOCK
    hg = MOBA_HEADS_PER_STEP
    wide = hg * HEAD_DIM
    return pl.pallas_call(
        _moba_prompt_kernel,
        grid=(batch, MOBA_HEADS // hg, nq),
        in_specs=[pl.BlockSpec((1, hg, LANES), lambda b, h, i: (h, 0, 0)),
                  pl.BlockSpec((MOBA_BLOCK, wide), lambda b, h, i: (b * nq + i, h)),
                  pl.BlockSpec((seq, wide), lambda b, h, i: (b, h)),
                  pl.BlockSpec((wide, seq), lambda b, h, i: (h, b))],
        out_specs=pl.BlockSpec((MOBA_BLOCK, wide), lambda b, h, i: (b * nq + i, h)),
        out_shape=jax.ShapeDtypeStruct((batch * seq, MOBA_W), f32),
        scratch_shapes=[pltpu.VMEM((hg, seq // MOBA_BLOCK, MOBA_BLOCK), f32)],
        compiler_params=_cparams(("arbitrary", "arbitrary", "arbitrary")),
        name="moba_prompt",
    )(_alibi_groups(MOBA_HEADS // hg, hg), qm, kvm, vmt)


CMP_HALF = NSA_CMP_LEN // 2
N_CMP_ROWS = 128


def _compress_rows(xa, xb, w1_ref, w2_ref):
    half = CMP_HALF * HEAD_DIM
    y = _dot(xa, w1_ref[0, :half, :].astype(bf16))
    z = _dot(xb, w1_ref[0, half:, :].astype(bf16))
    parts = []
    for r in range(y.shape[0] // N_CMP_ROWS):
        zr = z[r * N_CMP_ROWS:(r + 1) * N_CMP_ROWS]
        parts.append(y[r * N_CMP_ROWS:(r + 1) * N_CMP_ROWS] + pltpu.roll(zr, N_CMP_ROWS - 1, 0))
    hid = _gelu(jnp.concatenate(parts, axis=0))
    return _dot(hid.astype(bf16), w2_ref[0].astype(bf16))


def _cmp_prompt_kernel(x0_ref, x1_ref, pe_ref, w1_ref, w2_ref, o_ref):
    pe = pe_ref[0]
    xa, xb = [], []
    for x_ref in (x0_ref, x1_ref):
        pa, pb = [], []
        for l in range(CMP_HALF):
            xl = x_ref[pl.ds(l, N_CMP_ROWS, stride=CMP_HALF), :]
            pa.append((xl + pe[l:l + 1]).astype(bf16))
            pb.append((xl + pe[CMP_HALF + l:CMP_HALF + l + 1]).astype(bf16))
        xa.append(jnp.concatenate(pa, axis=1))
        xb.append(jnp.concatenate(pb, axis=1))
    out = _compress_rows(jnp.concatenate(xa, axis=0), jnp.concatenate(xb, axis=0), w1_ref, w2_ref)
    for g in range(NSA_KV_HEADS):
        o_ref[0, 0, g] = out[g * N_CMP_ROWS:(g + 1) * N_CMP_ROWS]


def compress_prompt(kvc, pe2, w12, w22, batch, seq):
    return pl.pallas_call(
        _cmp_prompt_kernel,
        grid=(batch, 2),
        in_specs=[pl.BlockSpec((seq, HEAD_DIM), lambda b, kv: (b, NSA_KV_HEADS * kv)),
                  pl.BlockSpec((seq, HEAD_DIM), lambda b, kv: (b, NSA_KV_HEADS * kv + 1)),
                  pl.BlockSpec((1, NSA_CMP_LEN, HEAD_DIM), lambda b, kv: (kv, 0, 0)),
                  pl.BlockSpec((1, NSA_CMP_LEN * HEAD_DIM, NSA_CMP_HIDDEN), lambda b, kv: (kv, 0, 0)),
                  pl.BlockSpec((1, NSA_CMP_HIDDEN, HEAD_DIM), lambda b, kv: (kv, 0, 0))],
        out_specs=pl.BlockSpec((1, 1, NSA_KV_HEADS, N_CMP_ROWS, HEAD_DIM), lambda b, kv: (b, kv, 0, 0, 0)),
        out_shape=jax.ShapeDtypeStruct((batch, 2, NSA_KV_HEADS, N_CMP_ROWS, HEAD_DIM), f32),
        compiler_params=_cparams(("arbitrary", "arbitrary")),
        name="compress_prompt",
    )(kvc, kvc, pe2, w12, w22)


NSA_TQ = 128
N_CMP = 127


def _masked_softmax(s, mask):
    s = jnp.where(mask, s, NEG)
    m = jnp.max(s, axis=-1, keepdims=True)
    m = jnp.where(m > NEG, m, 0.0)
    p = jnp.where(mask, jnp.exp(s - m), 0.0)
    return p / jnp.maximum(jnp.sum(p, axis=-1, keepdims=True), 1e-30)


def _overlap_matrix():
    c = lax.broadcasted_iota(jnp.int32, (LANES, LANES), 0)
    j = lax.broadcasted_iota(jnp.int32, (LANES, LANES), 1)
    cs = NSA_CMP_STRIDE * c
    bs = NSA_SEL_BLOCK * j
    return ((cs < bs + NSA_SEL_BLOCK) & (cs + NSA_CMP_LEN - 1 >= bs)).astype(f32)


NSA_TK = 256
N_SEL_BLOCKS = 32


def _nsa_prompt_kernel(slope_ref, q_ref, gt_ref, ck_ref, cvt_ref, ks_ref, vst_ref, kw_ref, vwt_ref, o_ref, sel_ref):
    i = pl.program_id(1)
    tq, tk, R, G, ls, dh = NSA_TQ, NSA_TK, NSA_GROUP, NSA_KV_HEADS, NSA_SEL_BLOCK, HEAD_DIM
    W = R * tq
    q_all = q_ref[...]
    gates_t = gt_ref[...].T
    n_lane = lax.broadcasted_iota(jnp.int32, (1, W), 1) & (tq - 1)
    t_lane = i * tq + n_lane
    t_q = i * tq + lax.broadcasted_iota(jnp.int32, (N_SEL_BLOCKS, tq), 1)
    jrow = lax.broadcasted_iota(jnp.int32, (N_SEL_BLOCKS, tq), 0)
    crow = lax.broadcasted_iota(jnp.int32, (N_CMP_ROWS, W), 0)
    visible = (NSA_CMP_STRIDE * crow + NSA_CMP_LEN - 1 <= t_lane) & (crow < N_CMP)
    oj = lax.broadcasted_iota(jnp.int32, (N_SEL_BLOCKS, N_CMP_ROWS), 0) * ls
    oc = lax.broadcasted_iota(jnp.int32, (N_SEL_BLOCKS, N_CMP_ROWS), 1) * NSA_CMP_STRIDE
    overlap_t = ((oc < oj + ls) & (oc + NSA_CMP_LEN - 1 >= oj)).astype(f32)
    krow = lax.broadcasted_iota(jnp.int32, (tk, W), 0)
    rel_i = n_lane - krow
    rel = rel_i.astype(f32)
    diag = (i * tq) // tk
    diag_off = pl.multiple_of(diag * tk, tk)

    def chosen_rows(g, kt):
        parts = [jnp.broadcast_to(sel_ref[g, pl.ds(kt * (tk // ls) + b, 1), :], (ls, W)) for b in range(tk // ls)]
        return jnp.concatenate(parts, axis=0) > 0.5

    groups = []
    for g in range(G):
        gs = slice(g * dh, (g + 1) * dh)
        qs = jnp.concatenate([q_all[:, (g * R + r) * dh:(g * R + r + 1) * dh] for r in range(R)], axis=0)
        slope = slope_ref[g]
        s_c = jnp.where(visible, _dot_nt(ck_ref[0, g], qs, HIGHEST) * SCALE, NEG)
        m_c = jnp.max(s_c, axis=0, keepdims=True)
        p_c = jnp.where(visible, jnp.exp(s_c - jnp.where(m_c > NEG, m_c, 0.0)), 0.0)
        p_c = p_c / jnp.maximum(jnp.sum(p_c, axis=0, keepdims=True), 1e-30)
        o_c = _dot(cvt_ref[0, g].astype(bf16), p_c.astype(bf16))
        p_sum = p_c[:, 0:tq]
        for r in range(1, R):
            p_sum = p_sum + p_c[:, r * tq:(r + 1) * tq]
        imp = _dot(overlap_t, p_sum, HIGHEST)
        sel = _topk_rows(jnp.where(jrow < t_q // ls, imp, NEG), NSA_SEL_TOPK)
        sel_ref[g] = jnp.concatenate([sel] * R, axis=1)
        qb = (qs * SCALE).astype(bf16)
        bias = slope * rel
        d0 = slope * (i * tq - diag * tk).astype(f32)
        dist = rel_i + (i * tq - diag * tk)
        key_blk = (diag * tk + krow) // ls
        s = _dot_nt(ks_ref[pl.ds(diag_off, tk), gs].astype(bf16), qb) - bias - d0
        ok = chosen_rows(g, diag) | ((key_blk == t_lane // ls) & (dist >= 0))
        slc0 = _softmax_first(jnp.where(ok, s, NEG), vst_ref[gs, pl.ds(diag_off, tk)].astype(bf16))
        s = _dot_nt(kw_ref[pl.ds(diag_off, tk), gs].astype(bf16), qb) - bias - d0
        win0 = _softmax_first(jnp.where((dist >= 0) & (dist < NSA_WINDOW), s, NEG),
                              vwt_ref[gs, pl.ds(diag_off, tk)].astype(bf16))
        groups.append((gs, qb, bias, slope, o_c, slc0, win0))

    def slc_body(kt, carry):
        off = pl.multiple_of(kt * tk, tk)
        out = []
        for g, (gs, qb, bias, slope, *_rest) in enumerate(groups):
            s = _dot_nt(ks_ref[pl.ds(off, tk), gs].astype(bf16), qb) - bias - slope * (i * tq - kt * tk).astype(f32)
            s = jnp.where(chosen_rows(g, kt), s, NEG)
            out.append(_softmax_next(s, vst_ref[gs, pl.ds(off, tk)].astype(bf16), *carry[g]))
        return tuple(out)

    def win_body(kt, carry):
        off = pl.multiple_of(kt * tk, tk)
        shift = i * tq - kt * tk
        out = []
        for g, (gs, qb, bias, slope, *_rest) in enumerate(groups):
            s = _dot_nt(kw_ref[pl.ds(off, tk), gs].astype(bf16), qb) - bias - slope * shift.astype(f32)
            s = jnp.where(rel_i + shift < NSA_WINDOW, s, NEG)
            out.append(_softmax_next(s, vwt_ref[gs, pl.ds(off, tk)].astype(bf16), *carry[g]))
        return tuple(out)

    slc = lax.fori_loop(0, diag, slc_body, tuple(grp[5] for grp in groups))
    first_win = jnp.maximum(i * tq - (NSA_WINDOW - 1), 0) // tk
    win = lax.fori_loop(first_win, diag, win_body, tuple(grp[6] for grp in groups))

    for g, (gs, qb, bias, slope, o_c, *_rest) in enumerate(groups):
        def gate_row(branch):
            return jnp.concatenate([gates_t[NSA_BRANCHES * (g * R + r) + branch:NSA_BRANCHES * (g * R + r) + branch + 1]
                                    for r in range(R)], axis=1)
        o_t = (gate_row(0) * o_c + gate_row(1) * (slc[g][2] / slc[g][1]) + gate_row(2) * (win[g][2] / win[g][1]))
        o = o_t.T
        for r in range(R):
            o_ref[:, (g * R + r) * dh:(g * R + r + 1) * dh] = o[r * tq:(r + 1) * tq]


def _alibi_groups(n_groups, group):
    n = n_groups * group
    s = (2.0 ** (-8.0 * np.arange(1, n + 1) / n)).reshape(n_groups, group)
    return jnp.asarray(np.broadcast_to(s[:, :, None], (n_groups, group, LANES)), dtype=f32)


def nsa_prompt(qn, gates, ck, cvt, kvs, vst, kvw, vwt, batch, seq):
    assert seq // NSA_SEL_BLOCK == N_SEL_BLOCKS and (seq - NSA_CMP_LEN) // NSA_CMP_STRIDE + 1 == N_CMP
    nq = seq // NSA_TQ
    G, R = NSA_KV_HEADS, NSA_GROUP
    n = G * R
    slopes = (2.0 ** (-8.0 * np.arange(1, n + 1) / n)).reshape(G, 1, R, 1)
    slope_lanes = jnp.asarray(np.broadcast_to(slopes, (G, 1, R, NSA_TQ)).reshape(G, 1, R * NSA_TQ), dtype=f32)
    full = lambda a: pl.BlockSpec(a.shape, lambda b, i: (0,) * a.ndim)
    return pl.pallas_call(
        _nsa_prompt_kernel,
        grid=(batch, nq),
        in_specs=[full(slope_lanes),
                  pl.BlockSpec((NSA_TQ, NSA_W), lambda b, i: (b * nq + i, 0)),
                  pl.BlockSpec((NSA_TQ, LANES), lambda b, i: (b * nq + i, 0)),
                  pl.BlockSpec((1, G, N_CMP_ROWS, HEAD_DIM), lambda b, i: (b, 0, 0, 0)),
                  pl.BlockSpec((1, G, HEAD_DIM, N_CMP_ROWS), lambda b, i: (b, 0, 0, 0)),
                  pl.BlockSpec((seq, NSA_KVW), lambda b, i: (b, 0)),
                  pl.BlockSpec((NSA_KVW, seq), lambda b, i: (0, b)),
                  pl.BlockSpec((seq, NSA_KVW), lambda b, i: (b, 0)),
                  pl.BlockSpec((NSA_KVW, seq), lambda b, i: (0, b))],
        out_specs=pl.BlockSpec((NSA_TQ, NSA_W), lambda b, i: (b * nq + i, 0)),
        out_shape=jax.ShapeDtypeStruct((batch * seq, NSA_W), f32),
        scratch_shapes=[pltpu.VMEM((G, N_SEL_BLOCKS, R * NSA_TQ), f32)],
        compiler_params=_cparams(("arbitrary", "arbitrary")),
        name="nsa_prompt",
    )(slope_lanes, qn, gates, ck, cvt, kvs, vst, kvw, vwt)


def _head_slopes(n):
    s = 2.0 ** (-8.0 * np.arange(1, n + 1) / n)
    return jnp.asarray(np.broadcast_to(s[:, None], (n, LANES)), dtype=f32)


def _moba_sample_kernel(pt_ref, slope_ref, q_ref, kvn_ref, *refs):
    page_refs, o_ref = refs[:-1], refs[-1]
    n_pages = len(page_refs)
    mb = MOBA_BLOCK
    ppb = mb // PAGE_SIZE
    nb = n_pages // ppb
    t_new = n_pages * PAGE_SIZE
    q = q_ref[0]
    slope = slope_ref[...]
    tok = lax.broadcasted_iota(jnp.int32, (mb, MOBA_HEADS, LANES), 0)
    ones = jnp.ones((HEAD_DIM, LANES), bf16)
    gates, ms, ls, os_ = [], [], [], []
    for j in range(nb):
        k = jnp.concatenate([page_refs[ppb * j + u][pl.ds(0, PAGE_SIZE, stride=2)] for u in range(ppb)], axis=0)
        v = jnp.concatenate([page_refs[ppb * j + u][pl.ds(1, PAGE_SIZE, stride=2)] for u in range(ppb)], axis=0)
        kmean = jnp.sum(k, axis=0) / mb
        gates.append(jnp.sum(q * kmean, axis=-1, keepdims=True))
        dist = (t_new - j * mb - tok).astype(f32)
        kq = (k * q[None]).reshape(mb * MOBA_HEADS, HEAD_DIM).astype(bf16)
        s = _dot(kq, ones).reshape(mb, MOBA_HEADS, LANES) * SCALE - slope[None] * dist
        m = jnp.max(s, axis=0)
        p = jnp.exp(s - m[None])
        ms.append(m)
        ls.append(jnp.sum(p, axis=0))
        os_.append(jnp.sum(p * v, axis=0))
    chosen = []
    for j in range(nb):
        rank = jnp.zeros_like(gates[j])
        for j2 in range(nb):
            if j2 != j:
                ahead = (gates[j2] >= gates[j]) if j2 < j else (gates[j2] > gates[j])
                rank = rank + ahead.astype(f32)
        chosen.append(rank < MOBA_TOPK)
    kn, vn = kvn_ref[0, 0], kvn_ref[0, 1]
    s_own = jnp.sum(q * kn, axis=-1, keepdims=True) * SCALE
    m_all = s_own
    for j in range(nb):
        m_all = jnp.maximum(m_all, jnp.where(chosen[j], ms[j], NEG))
    w_own = jnp.exp(s_own - m_all)
    l_all = w_own
    o_all = w_own * vn
    for j in range(nb):
        w = jnp.where(chosen[j], jnp.exp(ms[j] - m_all), 0.0)
        l_all = l_all + w * ls[j]
        o_all = o_all + w * os_[j]
    o_ref[0] = o_all / l_all


def moba_sample(q3, kvn4, cache3, page_table):
    nseq, n_pages = page_table.shape
    rows = 2 * PAGE_SIZE
    page_spec = lambda p: pl.BlockSpec((rows, MOBA_HEADS, HEAD_DIM), lambda b, pt: (pt[b, p], 0, 0))
    return pl.pallas_call(
        _moba_sample_kernel,
        grid_spec=pltpu.PrefetchScalarGridSpec(
            num_scalar_prefetch=1, grid=(nseq,),
            in_specs=[pl.BlockSpec((MOBA_HEADS, LANES), lambda b, pt: (0, 0)),
                      pl.BlockSpec((1, MOBA_HEADS, HEAD_DIM), lambda b, pt: (b, 0, 0)),
                      pl.BlockSpec((1, 2, MOBA_HEADS, HEAD_DIM), lambda b, pt: (b, 0, 0, 0))]
                     + [page_spec(p) for p in range(n_pages)],
            out_specs=pl.BlockSpec((1, MOBA_HEADS, HEAD_DIM), lambda b, pt: (b, 0, 0))),
        out_shape=jax.ShapeDtypeStruct((nseq, MOBA_HEADS, HEAD_DIM), f32),
        compiler_params=_cparams(("arbitrary",)),
        name="moba_sample",
    )(page_table, _head_slopes(MOBA_HEADS), q3, kvn4, *([cache3] * n_pages))


KV_ROWS = 2 * NSA_KV_HEADS


def _nsa_sample_cmp_kernel(pt_ref, q_ref, pe_ref, w1k_ref, w1v_ref, w2k_ref, w2v_ref, *refs):
    page_refs, (oc_ref, sel_ref) = refs[:-2], refs[-2:]
    n_pages = len(page_refs)
    per_page = PAGE_SIZE // CMP_HALF
    t_new = n_pages * PAGE_SIZE
    G, R = NSA_KV_HEADS, NSA_GROUP
    q = q_ref[0]
    row = lax.broadcasted_iota(jnp.int32, (G * R, LANES), 0)
    lane = lax.broadcasted_iota(jnp.int32, (G * R, LANES), 1)
    comp = []
    for kv, (w1_ref, w2_ref) in enumerate(((w1k_ref, w2k_ref), (w1v_ref, w2v_ref))):
        pe = pe_ref[kv]
        xa, xb = [], []
        for g in range(G):
            pa, pb = [], []
            for l in range(CMP_HALF):
                xl = jnp.concatenate(
                    [pr[pl.ds(KV_ROWS * l + G * kv + g, per_page, stride=KV_ROWS * CMP_HALF), :] for pr in page_refs], axis=0)
                pa.append((xl + pe[l:l + 1]).astype(bf16))
                pb.append((xl + pe[CMP_HALF + l:CMP_HALF + l + 1]).astype(bf16))
            xa.append(jnp.concatenate(pa, axis=1))
            xb.append(jnp.concatenate(pb, axis=1))
        comp.append(_compress_rows(jnp.concatenate(xa, axis=0), jnp.concatenate(xb, axis=0), w1_ref, w2_ref))
    ck, cv = comp
    n_cmp = (t_new + 1 - NSA_CMP_LEN) // NSA_CMP_STRIDE + 1
    visible = (lane < n_cmp) & (NSA_CMP_STRIDE * lane + NSA_CMP_LEN - 1 <= t_new)
    o_c = jnp.zeros((G * R, HEAD_DIM), f32)
    imp_rows = jnp.full((G * R, LANES), NEG, f32)
    overlap = _overlap_matrix()
    for g in range(G):
        ck_g = ck[g * N_CMP_ROWS:(g + 1) * N_CMP_ROWS]
        cv_g = cv[g * N_CMP_ROWS:(g + 1) * N_CMP_ROWS]
        mine = (row >= g * R) & (row < (g + 1) * R)
        p_c = _masked_softmax(_dot_nt(q, ck_g, HIGHEST) * SCALE, visible)
        o_c = jnp.where(mine, _dot(p_c.astype(bf16), cv_g.astype(bf16)), o_c)
        p_sum = jnp.sum(jnp.where(mine, p_c, 0.0), axis=0, keepdims=True)
        imp = _dot(jnp.broadcast_to(p_sum, (G * R, LANES)), overlap, HIGHEST)
        imp_rows = jnp.where((row == g) & (lane < t_new // NSA_SEL_BLOCK), imp, imp_rows)
    oc_ref[0] = o_c
    sel = _topk_mask(imp_rows, NSA_SEL_TOPK)
    out = jnp.zeros((G * R, LANES), jnp.int32)
    remaining = sel
    for s in range(NSA_SEL_TOPK):
        idx = jnp.min(jnp.where(remaining > 0.5, lane, LANES), axis=-1, keepdims=True)
        found = idx < LANES
        out = jnp.where(lane == s, jnp.where(found, idx, 0), out)
        out = jnp.where(lane == NSA_SEL_TOPK + s, found.astype(jnp.int32), out)
        remaining = jnp.where(lane == idx, 0.0, remaining)
    sel_ref[0] = out


def nsa_sample_cmp(q3, cache2, page_table, pe2, w1k, w1v, w2k, w2v):
    nseq, n_pages = page_table.shape
    rows = PAGE_SIZE * KV_ROWS
    page_spec = lambda p: pl.BlockSpec((rows, HEAD_DIM), lambda b, pt: (pt[b, p], 0))
    full = lambda a: pl.BlockSpec(a.shape, lambda b, pt: (0,) * a.ndim)
    return pl.pallas_call(
        _nsa_sample_cmp_kernel,
        grid_spec=pltpu.PrefetchScalarGridSpec(
            num_scalar_prefetch=1, grid=(nseq,),
            in_specs=[pl.BlockSpec((1, NSA_HEADS, HEAD_DIM), lambda b, pt: (b, 0, 0)),
                      full(pe2), full(w1k), full(w1v), full(w2k), full(w2v)]
                     + [page_spec(p) for p in range(n_pages)],
            out_specs=[pl.BlockSpec((1, NSA_HEADS, HEAD_DIM), lambda b, pt: (b, 0, 0)),
                       pl.BlockSpec((1, NSA_HEADS, LANES), lambda b, pt: (b, 0, 0))]),
        out_shape=[jax.ShapeDtypeStruct((nseq, NSA_HEADS, HEAD_DIM), f32),
                   jax.ShapeDtypeStruct((nseq, NSA_HEADS, LANES), jnp.int32)],
        compiler_params=_cparams(("arbitrary",)),
        name="nsa_sample_cmp",
    )(page_table, q3, pe2, w1k, w1v, w2k, w2v, *([cache2] * n_pages))


def _decode_attend(q, slope, keys, vals, pos, valid, k_own, v_own, t_new):
    s = _dot_nt(q.astype(bf16), keys.astype(bf16)) * SCALE - slope * (t_new - pos).astype(f32)
    s = jnp.where(valid, s, NEG)
    s_own = jnp.sum(q * k_own, axis=-1, keepdims=True) * SCALE
    m = jnp.maximum(jnp.max(s, axis=-1, keepdims=True), s_own)
    p = jnp.exp(s - m)
    p_own = jnp.exp(s_own - m)
    denom = jnp.sum(p, axis=-1, keepdims=True) + p_own
    return (_dot(p.astype(bf16), vals.astype(bf16)) + p_own * v_own) / denom


def _nsa_sample_attn_kernel(pt_ref, sel_ref, slope_ref, q_ref, gt_ref, oc_ref, ksn_ref, kwn_ref, win_ref, *refs):
    blk_refs, (o_ref, wout_ref) = refs[:-2], refs[-2:]
    b = pl.program_id(0)
    G, R, ls, K = NSA_KV_HEADS, NSA_GROUP, NSA_SEL_BLOCK, NSA_SEL_TOPK
    t_new = pt_ref.shape[1] * PAGE_SIZE
    q = q_ref[0]
    slope = slope_ref[:, :1]
    row = lax.broadcasted_iota(jnp.int32, (G * R, HEAD_DIM), 0)
    own_rows = lambda ref, kv: jnp.where(row < R, ref[0, G * kv:G * kv + 1], ref[0, G * kv + 1:G * kv + 2])
    lane_s = lax.broadcasted_iota(jnp.int32, (1, K * ls), 1)
    n_win = win_ref.shape[0] // KV_ROWS
    lane_w = lax.broadcasted_iota(jnp.int32, (1, n_win), 1)
    pos_w = t_new - n_win + lane_w
    o_s = jnp.zeros((G * R, HEAD_DIM), f32)
    o_w = jnp.zeros((G * R, HEAD_DIM), f32)
    for g in range(G):
        keys = jnp.concatenate([blk_refs[g * K + s][pl.ds(g, ls, stride=KV_ROWS), :] for s in range(K)], axis=0)
        vals = jnp.concatenate([blk_refs[g * K + s][pl.ds(G + g, ls, stride=KV_ROWS), :] for s in range(K)], axis=0)
        pos = jnp.zeros((1, K * ls), jnp.int32)
        valid = jnp.zeros((1, K * ls), jnp.int32)
        for s in range(K):
            here = lane_s // ls == s
            pos = jnp.where(here, sel_ref[b, g * 2 * K + s] * ls + lane_s - s * ls, pos)
            valid = jnp.where(here, sel_ref[b, g * 2 * K + K + s], valid)
        mine = (row >= g * R) & (row < (g + 1) * R)
        o_s = jnp.where(mine, _decode_attend(q, slope, keys, vals, pos, valid > 0, own_rows(ksn_ref, 0), own_rows(ksn_ref, 1), t_new), o_s)
        keys_w = win_ref[pl.ds(g, n_win, stride=KV_ROWS), :]
        vals_w = win_ref[pl.ds(G + g, n_win, stride=KV_ROWS), :]
        ok_w = (pos_w > t_new - NSA_WINDOW) & (pos_w >= 0)
        o_w = jnp.where(mine, _decode_attend(q, slope, keys_w, vals_w, pos_w, ok_w, own_rows(kwn_ref, 0), own_rows(kwn_ref, 1), t_new), o_w)
    gt = gt_ref[0]
    o_ref[0] = gt[:, 0:1] * oc_ref[0] + gt[:, 1:2] * o_s + gt[:, 2:3] * o_w
    total = win_ref.shape[0]
    shifted = pltpu.roll(win_ref[...], total - KV_ROWS, 0)
    new8 = jnp.concatenate([kwn_ref[0], kwn_ref[0]], axis=0)
    row8 = lax.broadcasted_iota(jnp.int32, (2 * KV_ROWS, HEAD_DIM), 0)
    wout_ref[pl.ds(0, total - 2 * KV_ROWS), :] = shifted[:total - 2 * KV_ROWS]
    wout_ref[pl.ds(total - 2 * KV_ROWS, 2 * KV_ROWS), :] = jnp.where(row8 >= KV_ROWS, new8, shifted[total - 2 * KV_ROWS:])


def nsa_sample_attn(q3, gates3, o_c, ksn, kwn, slc2, win2, page_table, sel_flat):
    nseq = page_table.shape[0]
    G, K, ls = NSA_KV_HEADS, NSA_SEL_TOPK, NSA_SEL_BLOCK
    blocks_per_page = PAGE_SIZE // ls
    n_blocks = page_table.shape[1] * blocks_per_page
    win_rows = win2.shape[0] // nseq

    def blk_spec(g, s):
        def index(b, pt, sel):
            bb = jnp.minimum(b, nseq - 1)
            blk = jnp.clip(sel[bb, g * 2 * K + s], 0, n_blocks - 1)
            return (pt[bb, blk // blocks_per_page] * blocks_per_page + blk % blocks_per_page, 0)
        return pl.BlockSpec((ls * KV_ROWS, HEAD_DIM), index)

    per_seq = lambda shape: pl.BlockSpec((1,) + shape, lambda b, pt, sel: (b,) + (0,) * len(shape))
    return pl.pallas_call(
        _nsa_sample_attn_kernel,
        grid_spec=pltpu.PrefetchScalarGridSpec(
            num_scalar_prefetch=2, grid=(nseq,),
            in_specs=[pl.BlockSpec((NSA_HEADS, LANES), lambda b, pt, sel: (0, 0)),
                      per_seq((NSA_HEADS, HEAD_DIM)), per_seq((NSA_HEADS, LANES)), per_seq((NSA_HEADS, HEAD_DIM)),
                      per_seq((KV_ROWS, HEAD_DIM)), per_seq((KV_ROWS, HEAD_DIM)),
                      pl.BlockSpec((win_rows, HEAD_DIM), lambda b, pt, sel: (b, 0))]
                     + [blk_spec(g, s) for g in range(G) for s in range(K)],
            out_specs=[per_seq((NSA_HEADS, HEAD_DIM)),
                       pl.BlockSpec((win_rows, HEAD_DIM), lambda b, pt, sel: (b, 0))]),
        out_shape=[jax.ShapeDtypeStruct((nseq, NSA_HEADS, HEAD_DIM), f32),
                   jax.ShapeDtypeStruct(win2.shape, f32)],
        compiler_params=_cparams(("arbitrary",)),
        name="nsa_sample_attn",
    )(page_table, sel_flat, _head_slopes(NSA_HEADS), q3, gates3, o_c, ksn, kwn, win2, *([slc2] * (G * K)))


def _mid_kernel(x_ref, om_ref, on_ref, w_ref, g_ref, gate_ref, sh_ref, sc_ref, x1_ref, h2_ref):
    proj = (_dot(om_ref[...].astype(bf16), w_ref[:MOBA_W, :]) + _dot(on_ref[...].astype(bf16), w_ref[MOBA_W:, :]))
    x1 = x_ref[...] + gate_ref[0] * proj
    x1_ref[...] = x1
    h2_ref[...] = _rms_mod(x1, g_ref[...], sh_ref[0], sc_ref[0]).astype(bf16)


def mid_block(x, o_m, o_n, w_out_b, g_norm2, mods3, tm, rows_per_mod):
    t = x.shape[0]
    r = mods3.shape[1]
    tiles_per_mod = rows_per_mod // tm
    mod_spec = lambda which: pl.BlockSpec((1, r, D_MODEL), lambda i: (i // tiles_per_mod, 0, which))
    return pl.pallas_call(
        _mid_kernel,
        grid=(t // tm,),
        in_specs=[pl.BlockSpec((tm, D_MODEL), lambda i: (i, 0)),
                  pl.BlockSpec((tm, MOBA_W), lambda i: (i, 0)),
                  pl.BlockSpec((tm, NSA_W), lambda i: (i, 0)),
                  pl.BlockSpec((MOBA_W + NSA_W, D_MODEL), lambda i: (0, 0)),
                  pl.BlockSpec((1, D_MODEL), lambda i: (0, 0)),
                  mod_spec(2), mod_spec(3), mod_spec(4)],
        out_specs=[pl.BlockSpec((tm, D_MODEL), lambda i: (i, 0)),
                   pl.BlockSpec((tm, D_MODEL), lambda i: (i, 0))],
        out_shape=[jax.ShapeDtypeStruct((t, D_MODEL), f32), jax.ShapeDtypeStruct((t, D_MODEL), bf16)],
        compiler_params=_cparams(("arbitrary",)),
        name="mid_block",
    )(x, o_m, o_n, w_out_b, g_norm2.reshape(1, D_MODEL), mods3, mods3, mods3)


PEER_HALF = PEER_QDIM // 2
PEER_A_FULL = 8


def _top_values(s, k):
    tops = []
    cur = s
    for _ in range(k):
        m = jnp.max(cur, axis=0, keepdims=True)
        tops.append(m)
        cur = jnp.where(cur == m, NEG, cur)
    return jnp.concatenate(tops, axis=0)


def _peer_route_kernel(h_ref, wq_ref, keys_ref, s1_ref, thr_ref, e1_ref, coef_ref):
    q = _dot(h_ref[...], wq_ref[...])
    k = PEER_TOPK
    for h in range(PEER_HEADS):
        base = h * PEER_QDIM
        s0 = _dot_nt(keys_ref[h, 0], q[:, base:base + PEER_HALF], HIGHEST)
        s1 = _dot_nt(keys_ref[h, 1], q[:, base + PEER_HALF:base + PEER_QDIM], HIGHEST)
        top0 = _top_values(s0, k)
        top1 = _top_values(s1, k)
        cand = jnp.concatenate([top0[a:a + 1] + top1 for a in range(PEER_A_FULL)]
                               + [top0[PEER_A_FULL:] + top1[0:1]], axis=0)
        best = _top_values(cand, k)
        tau = best[k - 1:k]
        z = jnp.sum(jnp.exp(best - best[0:1]), axis=0, keepdims=True)
        thr = jnp.full(s0.shape, jnp.inf, f32)
        for a in range(k):
            thr_a = jnp.min(jnp.where(top0[a:a + 1] + top1 >= tau, top1, jnp.inf), axis=0, keepdims=True)
            thr = jnp.where(s0 == top0[a:a + 1], thr_a, thr)
        s1_ref[h] = s1
        thr_ref[h] = thr
        e1_ref[h] = jnp.exp(s1 - top1[0:1])
        coef_ref[h] = jnp.exp(s0 - top0[0:1]) / z


def peer_route(h2, wq_b, keys, tm):
    t = h2.shape[0]
    out_spec = pl.BlockSpec((PEER_HEADS, PEER_KEYS, tm), lambda i: (0, 0, i))
    out_shape = jax.ShapeDtypeStruct((PEER_HEADS, PEER_KEYS, t), f32)
    return pl.pallas_call(
        _peer_route_kernel,
        grid=(t // tm,),
        in_specs=[pl.BlockSpec((tm, D_MODEL), lambda i: (i, 0)),
                  pl.BlockSpec(wq_b.shape, lambda i: (0, 0)),
                  pl.BlockSpec(keys.shape, lambda i: (0, 0, 0, 0))],
        out_specs=[out_spec] * 4,
        out_shape=[out_shape] * 4,
        compiler_params=_cparams(("arbitrary",)),
        name="peer_route",
    )(h2, wq_b, keys)


PEER_EXPERT_TILE = 1024
PEER_ACC_COLS = 256


def _peer_expert_kernel(ht_ref, u_ref, vt_ref, s1_ref, thr_ref, e1_ref, coef_ref, x1_ref, gate_ref, gf_ref,
                        y_ref, acc_ref, p_ref):
    e = pl.program_id(1)
    te = u_ref.shape[0]
    rows_per_step = te // PEER_KEYS

    @pl.when(e == 0)
    def _():
        acc_ref[...] = jnp.zeros_like(acc_ref)

    act = _gelu(_dot(u_ref[...], ht_ref[...]))
    for r in range(rows_per_step):
        i0 = e * rows_per_step + r
        rs = slice(r * PEER_KEYS, (r + 1) * PEER_KEYS)
        for c in range(act.shape[1] // LANES):
            cs = slice(c * LANES, (c + 1) * LANES)
            w = jnp.zeros((PEER_KEYS, LANES), f32)
            for h in range(PEER_HEADS):
                thr = thr_ref[h, pl.ds(i0, 1), :][:, cs]
                coef = coef_ref[h, pl.ds(i0, 1), :][:, cs]
                w = w + jnp.where(s1_ref[h, :, cs] >= thr, e1_ref[h, :, cs] * coef, 0.0)
            p_ref[rs, cs] = (w * act[rs, cs]).astype(bf16)
    cols = min(PEER_ACC_COLS, act.shape[1])
    for c in range(act.shape[1] // cols):
        cs = slice(c * cols, (c + 1) * cols)
        acc_ref[:, cs] += _dot(vt_ref[...], p_ref[:, cs])

    @pl.when(e == pl.num_programs(1) - 1)
    def _():
        y = x1_ref[...] + gate_ref[0] * acc_ref[...].T
        y_ref[...] = (y * lax.rsqrt(jnp.mean(y * y, axis=-1, keepdims=True) + RMS_EPS)) * gf_ref[...]


def peer_experts(h2t, u_b, vt_b, route, x1, mods3, g_final, tm, te, rows_per_mod):
    t = h2t.shape[1]
    r = mods3.shape[1]
    n_exp = u_b.shape[0]
    tiles_per_mod = rows_per_mod // tm
    once = dict(pipeline_mode=pl.Buffered(1))
    route_spec = pl.BlockSpec((PEER_HEADS, PEER_KEYS, tm), lambda i, e: (0, 0, i), **once)
    return pl.pallas_call(
        _peer_expert_kernel,
        grid=(t // tm, n_exp // te),
        in_specs=[pl.BlockSpec((D_MODEL, tm), lambda i, e: (0, i), **once),
                  pl.BlockSpec((te, D_MODEL), lambda i, e: (e, 0)),
                  pl.BlockSpec((D_MODEL, te), lambda i, e: (0, e)),
                  route_spec, route_spec, route_spec, route_spec,
                  pl.BlockSpec((tm, D_MODEL), lambda i, e: (i, 0), **once),
                  pl.BlockSpec((1, r, D_MODEL), lambda i, e: (i // tiles_per_mod, 0, 5)),
                  pl.BlockSpec((1, D_MODEL), lambda i, e: (0, 0))],
        out_specs=pl.BlockSpec((tm, D_MODEL), lambda i, e: (i, 0)),
        out_shape=jax.ShapeDtypeStruct((t, D_MODEL), f32),
        scratch_shapes=[pltpu.VMEM((D_MODEL, tm), f32), pltpu.VMEM((te, tm), bf16)],
        compiler_params=_cparams(("arbitrary", "arbitrary")),
        name="peer_experts",
    )(h2t, u_b, vt_b, *route, x1, mods3, g_final.reshape(1, D_MODEL))


def _group_forward(x2, mods3, tm, rows_per_mod, w, attend):
    proj = in_projection(x2, mods3, w["g_norm1"], w["w_in"], tm, rows_per_mod)
    o_m, o_n = attend(proj)
    x1, h2 = mid_block(x2, o_m, o_n, w["w_out"], w["g_norm2"], mods3, min(tm, 256), rows_per_mod)
    route = peer_route(h2, w["peer_w_q"], w["peer_keys"], min(tm, 256))
    y = peer_experts(h2.T, w["peer_u"], w["peer_vt"], route, x1, mods3, w["g_final"], tm, PEER_EXPERT_TILE, rows_per_mod)
    return proj, y


def kernel(x_prompt, x_sample, cache_moba_kv, cache_nsa_cmp_kv, cache_nsa_slc_kv, state_nsa_win_kv, page_table,
           c_prompt, c_sample, w_ada, b_ada, g_norm1, w_in, cmp_pe_k, cmp_w1_k, cmp_w2_k, cmp_pe_v, cmp_w1_v,
           cmp_w2_v, w_out, g_norm2, peer_w_q, peer_keys, peer_u, peer_v, g_final):
    assert w_ada.shape[0] == 1, "single layer"
    batch, seq, _ = x_prompt.shape
    nseq, dec_seq, _ = x_sample.shape
    assert dec_seq == 1 and state_nsa_win_kv.shape[2] == NSA_WINDOW and seq >= NSA_WINDOW
    G, H, dh = NSA_KV_HEADS, MOBA_HEADS, HEAD_DIM

    c_all = jnp.concatenate([c_prompt, c_sample], axis=0)
    pad = (-c_all.shape[0]) % 8
    mods = ada_mods(jnp.pad(c_all, ((0, pad), (0, 0))), w_ada[0], b_ada[0])
    mods_p = mods[:batch].reshape(batch, 1, N_MOD * D_MODEL)
    mods_s = mods[batch:batch + nseq].reshape(1, nseq, N_MOD * D_MODEL)

    p_in = w_in.shape[2]
    pe2 = jnp.stack([cmp_pe_k[0], cmp_pe_v[0]])
    w12 = jnp.stack([cmp_w1_k[0], cmp_w1_v[0]]).astype(bf16)
    w22 = jnp.stack([cmp_w2_k[0], cmp_w2_v[0]]).astype(bf16)
    w = dict(
        g_norm1=g_norm1[0], g_norm2=g_norm2[0], g_final=g_final,
        w_in=jnp.pad(w_in[0], ((0, 0), (0, IN_COLS - p_in))).astype(bf16),
        w_out=w_out[0].astype(bf16),
        peer_w_q=peer_w_q[0].astype(bf16), peer_keys=peer_keys[0],
        peer_u=peer_u[0].astype(bf16), peer_vt=peer_v[0].T.astype(bf16),
    )

    def attend_prompt(proj):
        qm, kvm, qn, kvc, kvs, kvw, gates = proj
        ckv = compress_prompt(kvc, pe2, w12, w22, batch, seq)
        o_m = moba_prompt(qm, kvm, kvm[:, MOBA_W:].T, batch, seq)
        o_n = nsa_prompt(qn, gates, ckv[:, 0], jnp.swapaxes(ckv[:, 1], 2, 3),
                         kvs, kvs[:, NSA_KVW:].T, kvw, kvw[:, NSA_KVW:].T, batch, seq)
        return o_m, o_n

    win_out = []

    def attend_sample(proj):
        qm, kvm, qn, kvc, kvs, kvw, gates = proj
        o_m = moba_sample(qm.reshape(nseq, H, dh), kvm.reshape(nseq, 2, H, dh), cache_moba_kv.reshape(-1, H, dh), page_table)
        qn3 = qn.reshape(nseq, NSA_HEADS, dh)
        o_c, sel = nsa_sample_cmp(qn3, cache_nsa_cmp_kv.reshape(-1, dh), page_table, pe2,
                                  w12[0:1], w12[1:2], w22[0:1], w22[1:2])
        sel_flat = sel[:, :G, :2 * NSA_SEL_TOPK].reshape(nseq, G * 2 * NSA_SEL_TOPK)
        gates3 = jnp.pad(gates[:, :NSA_HEADS * NSA_BRANCHES].reshape(nseq, NSA_HEADS, NSA_BRANCHES),
                         ((0, 0), (0, 0), (0, LANES - NSA_BRANCHES)))
        o_n, wout = nsa_sample_attn(qn3, gates3, o_c, kvs.reshape(nseq, KV_ROWS, dh), kvw.reshape(nseq, KV_ROWS, dh),
                                    cache_nsa_slc_kv.reshape(-1, dh), state_nsa_win_kv.reshape(-1, dh), page_table, sel_flat)
        win_out.append(wout)
        return o_m.reshape(nseq, MOBA_W), o_n.reshape(nseq, NSA_W)

    proj_p, y_p = _group_forward(x_prompt.reshape(batch * seq, D_MODEL), mods_p, 512, seq, w, attend_prompt)
    proj_s, y_s = _group_forward(x_sample.reshape(nseq, D_MODEL), mods_s, nseq, nseq, w, attend_sample)

    kv_p = lambda a, nh: a.reshape(1, batch, seq, 2, nh, dh)
    kv_s = lambda a, nh: a.reshape(1, nseq, 1, 2, nh, dh)
    return (y_p.reshape(batch, seq, D_MODEL), y_s.reshape(nseq, 1, D_MODEL),
            kv_p(proj_p[1], H), kv_s(proj_s[1], H),
            kv_p(proj_p[3], G), kv_s(proj_s[3], G),
            kv_p(proj_p[4], G), kv_s(proj_s[4], G),
            kv_p(proj_p[5], G)[:, :, seq - NSA_WINDOW:],
            win_out[0].reshape(state_nsa_win_kv.shape))
```

```python
import functools

import jax
import jax.numpy as jnp
import numpy as np
from jax import lax
from jax.experimental import pallas as pl
from jax.experimental.pallas import tpu as pltpu

f32 = jnp.float32
bf16 = jnp.bfloat16
HIGHEST = lax.Precision.HIGHEST

D_MODEL = 2048
HEAD_DIM = 128
MOBA_HEADS = 8
NSA_HEADS = 8
NSA_KV_HEADS = 2
NSA_GROUP = 4
MOBA_W = MOBA_HEADS * HEAD_DIM
NSA_W = NSA_HEADS * HEAD_DIM
NSA_KVW = NSA_KV_HEADS * HEAD_DIM
MOBA_BLOCK = 256
MOBA_TOPK = 3
NSA_CMP_LEN = 32
NSA_CMP_STRIDE = 16
NSA_CMP_HIDDEN = 256
NSA_SEL_BLOCK = 64
NSA_SEL_TOPK = 4
NSA_WINDOW = 512
NSA_BRANCHES = 3
PEER_KEYS = 128
PEER_HEADS = 8
PEER_QDIM = 256
PEER_TOPK = 16
N_MOD = 6
RMS_EPS = 1e-6
PAGE_SIZE = 128
SCALE = HEAD_DIM ** -0.5
NEG = -jnp.inf
LANES = 128

IN_TILE = 512
IN_WIDE_TILES = 11
IN_COLS = IN_WIDE_TILES * IN_TILE + LANES
VMEM_LIMIT = 56 * 1024 * 1024


def _cparams(sem):
    return pltpu.CompilerParams(dimension_semantics=sem, vmem_limit_bytes=VMEM_LIMIT)


def _gelu(x):
    z2 = np.float32(2.0 * np.sqrt(2.0 / np.pi)) * (x + 0.044715 * (x * x * x))
    return x / (1.0 + jnp.exp(-z2))


def _dot_nt(a, b, precision=None):
    return lax.dot_general(a, b, (((1,), (1,)), ((), ())), precision=precision, preferred_element_type=f32)


def _dot(a, b, precision=None):
    return jnp.dot(a, b, precision=precision, preferred_element_type=f32)


def _ada_kernel(c_ref, w_ref, b_ref, o_ref):
    c = c_ref[...]
    a = c * jax.nn.sigmoid(c)
    o_ref[...] = _dot(a, w_ref[...], HIGHEST) + b_ref[...]


def ada_mods(c, w_ada, b_ada):
    rows = c.shape[0]
    n = w_ada.shape[1]
    tn = 1024
    return pl.pallas_call(
        _ada_kernel,
        grid=(n // tn,),
        in_specs=[pl.BlockSpec((rows, D_MODEL), lambda j: (0, 0)),
                  pl.BlockSpec((D_MODEL, tn), lambda j: (0, j)),
                  pl.BlockSpec((1, tn), lambda j: (0, j))],
        out_specs=pl.BlockSpec((rows, tn), lambda j: (0, j)),
        out_shape=jax.ShapeDtypeStruct((rows, n), f32),
        compiler_params=_cparams(("arbitrary",)),
        name="ada_mods",
    )(c, w_ada, b_ada.reshape(1, n))


def _rms_mod(x, g, shift, scale):
    y = x * lax.rsqrt(jnp.mean(x * x, axis=-1, keepdims=True) + RMS_EPS)
    return (y * g) * (1.0 + scale) + shift


def _inproj_kernel(x_ref, g_ref, sh_ref, sc_ref, w_ref, wg_ref,
                   qm_ref, kvm_ref, qn_ref, kvc_ref, kvs_ref, kvw_ref, gt_ref,
                   kvm_out_ref, kvc_out_ref, kvs_out_ref, kvw_out_ref, h_scr):
    j = pl.program_id(1)
    tm = x_ref.shape[0]
    heads_per_tile = IN_TILE // HEAD_DIM

    @pl.when(j == 0)
    def _():
        h = _rms_mod(x_ref[...], g_ref[...], sh_ref[0], sc_ref[0]).astype(bf16)
        h_scr[...] = h
        gt_ref[...] = jax.nn.sigmoid(_dot(h, wg_ref[...]))

    acc = _dot(h_scr[...], w_ref[...])

    @pl.when(j < 2)
    def _():
        qm_ref[...] = acc

    for p in range(2 * MOBA_W // IN_TILE):
        @pl.when(j == 2 + p)
        def _(p=p):
            kvm_ref[...] = acc
            kvm_out_ref[:, heads_per_tile * p:heads_per_tile * (p + 1), :] = acc.reshape(tm, heads_per_tile, HEAD_DIM)

    @pl.when((j >= 6) & (j < 8))
    def _():
        qn_ref[...] = acc

    for p, (flat_ref, out_ref) in enumerate(((kvc_ref, kvc_out_ref), (kvs_ref, kvs_out_ref), (kvw_ref, kvw_out_ref))):
        @pl.when(j == 8 + p)
        def _(flat_ref=flat_ref, out_ref=out_ref):
            flat_ref[...] = acc
            out_ref[...] = acc.reshape(tm, 2, NSA_KV_HEADS, HEAD_DIM)


def in_projection(x, mods3, g_norm1, w_in_p, tm, rows_per_mod):
    t = x.shape[0]
    r = mods3.shape[1]
    tiles_per_mod = rows_per_mod // tm
    mod_spec = lambda which: pl.BlockSpec((1, r, D_MODEL), lambda i, j: (i // tiles_per_mod, 0, which))
    clip = lambda j, lo, n: jnp.clip(j - lo, 0, n - 1)
    out_shapes = [jax.ShapeDtypeStruct((t, w), f32) for w in (MOBA_W, 2 * MOBA_W, NSA_W, 2 * NSA_KVW, 2 * NSA_KVW, 2 * NSA_KVW, LANES)]
    out_specs = [
        pl.BlockSpec((tm, IN_TILE), lambda i, j: (i, clip(j, 0, 2))),
        pl.BlockSpec((tm, IN_TILE), lambda i, j: (i, clip(j, 2, 4))),
        pl.BlockSpec((tm, IN_TILE), lambda i, j: (i, clip(j, 6, 2))),
        pl.BlockSpec((tm, IN_TILE), lambda i, j: (i, 0)),
        pl.BlockSpec((tm, IN_TILE), lambda i, j: (i, 0)),
        pl.BlockSpec((tm, IN_TILE), lambda i, j: (i, 0)),
        pl.BlockSpec((tm, LANES), lambda i, j: (i, 0)),
        pl.BlockSpec((tm, 2 * MOBA_HEADS, HEAD_DIM), lambda i, j: (i, 0, 0)),
    ] + [pl.BlockSpec((tm, 2, NSA_KV_HEADS, HEAD_DIM), lambda i, j: (i, 0, 0, 0))] * 3
    out_shapes += [jax.ShapeDtypeStruct((t, 2 * MOBA_HEADS, HEAD_DIM), f32)]
    out_shapes += [jax.ShapeDtypeStruct((t, 2, NSA_KV_HEADS, HEAD_DIM), f32)] * 3
    return pl.pallas_call(
        _inproj_kernel,
        grid=(t // tm, IN_WIDE_TILES),
        in_specs=[pl.BlockSpec((tm, D_MODEL), lambda i, j: (i, 0)),
                  pl.BlockSpec((1, D_MODEL), lambda i, j: (0, 0)),
                  mod_spec(0), mod_spec(1),
                  pl.BlockSpec((D_MODEL, IN_TILE), lambda i, j: (0, j)),
                  pl.BlockSpec((D_MODEL, LANES), lambda i, j: (0, IN_WIDE_TILES * IN_TILE // LANES))],
        out_specs=out_specs,
        out_shape=out_shapes,
        scratch_shapes=[pltpu.VMEM((tm, D_MODEL), bf16)],
        compiler_params=_cparams(("arbitrary", "arbitrary")),
        name="in_projection",
    )(x, g_norm1.reshape(1, D_MODEL), mods3, mods3, w_in_p, w_in_p)


def _topk_rows(score, k):
    rows = score.shape[0]
    sub = lax.broadcasted_iota(jnp.int32, score.shape, 0)
    sel = jnp.zeros(score.shape, f32)
    g = score
    for _ in range(k):
        m = jnp.max(g, axis=0, keepdims=True)
        hit = (g == m) & (m > NEG)
        idx = jnp.min(jnp.where(hit, sub, rows), axis=0, keepdims=True)
        pick = sub == idx
        sel = jnp.where(pick, 1.0, sel)
        g = jnp.where(pick, NEG, g)
    return sel


def _softmax_first(s, vt):
    m = jnp.max(s, axis=0, keepdims=True)
    p = jnp.exp(s - m)
    return m, jnp.sum(p, axis=0, keepdims=True), _dot(vt, p.astype(bf16))


def _softmax_next(s, vt, m_i, l_i, acc):
    m_new = jnp.maximum(m_i, jnp.max(s, axis=0, keepdims=True))
    alpha = jnp.exp(m_i - m_new)
    p = jnp.exp(s - m_new)
    return m_new, alpha * l_i + jnp.sum(p, axis=0, keepdims=True), alpha * acc + _dot(vt, p.astype(bf16))


MOBA_HEADS_PER_STEP = 4


def _moba_prompt_kernel(slope_ref, q_ref, k_ref, vt_ref, o_ref, sel_ref):
    i = pl.program_id(2)
    mb, dh, hg = MOBA_BLOCK, HEAD_DIM, MOBA_HEADS_PER_STEP
    nb = k_ref.shape[0] // mb
    krow = lax.broadcasted_iota(jnp.int32, (mb, mb), 0)
    qcol = lax.broadcasted_iota(jnp.int32, (mb, mb), 1)
    rel = (qcol - krow).astype(f32)
    blk = lax.broadcasted_iota(jnp.int32, (nb, mb), 0)
    own = pl.multiple_of(i * mb, mb)
    heads = []
    for hh in range(hg):
        cs = slice(hh * dh, (hh + 1) * dh)
        q = q_ref[:, cs]
        slope = slope_ref[0, hh:hh + 1, :1]
        kmean = jnp.concatenate(
            [jnp.mean(k_ref[pl.ds(j * mb, mb), cs], axis=0, keepdims=True) for j in range(nb)], axis=0)
        gate = _dot_nt(kmean, q, HIGHEST)
        sel_ref[hh] = _topk_rows(jnp.where(blk < i, gate, NEG), MOBA_TOPK)
        qb = (q * SCALE).astype(bf16)
        bias = slope * rel
        s = _dot_nt(k_ref[pl.ds(own, mb), cs].astype(bf16), qb) - bias
        s = jnp.where(krow <= qcol, s, NEG)
        heads.append((cs, qb, bias, slope, _softmax_first(s, vt_ref[cs, pl.ds(own, mb)].astype(bf16))))

    def body(j, carry):
        off = pl.multiple_of(j * mb, mb)
        out = []
        for hh, (cs, qb, bias, slope, _) in enumerate(heads):
            s = _dot_nt(k_ref[pl.ds(off, mb), cs].astype(bf16), qb) - bias - slope * ((i - j) * mb).astype(f32)
            s = jnp.where(sel_ref[hh, pl.ds(j, 1), :] > 0.5, s, NEG)
            out.append(_softmax_next(s, vt_ref[cs, pl.ds(off, mb)].astype(bf16), *carry[hh]))
        return tuple(out)

    final = lax.fori_loop(0, i, body, tuple(h[4] for h in heads))
    for hh, (cs, *_rest) in enumerate(heads):
        _, l_i, acc = final[hh]
        o_ref[:, cs] = (acc / l_i).T


def moba_prompt(qm, kvm, vmt, batch, seq):
    nq = seq // MOBA_BLOCK
    hg = MOBA_HEADS_PER_STEP
    wide = hg * HEAD_DIM
    return pl.pallas_call(
        _moba_prompt_kernel,
        grid=(batch, MOBA_HEADS // hg, nq),
        in_specs=[pl.BlockSpec((1, hg, LANES), lambda b, h, i: (h, 0, 0)),
                  pl.BlockSpec((MOBA_BLOCK, wide), lambda b, h, i: (b * nq + i, h)),
                  pl.BlockSpec((seq, wide), lambda b, h, i: (b, h)),
                  pl.BlockSpec((wide, seq), lambda b, h, i: (h, b))],
        out_specs=pl.BlockSpec((MOBA_BLOCK, wide), lambda b, h, i: (b * nq + i, h)),
        out_shape=jax.ShapeDtypeStruct((batch * seq, MOBA_W), f32),
        scratch_shapes=[pltpu.VMEM((hg, seq // MOBA_BLOCK, MOBA_BLOCK), f32)],
        compiler_params=_cparams(("arbitrary", "arbitrary", "arbitrary")),
        name="moba_prompt",
    )(_alibi_groups(MOBA_HEADS // hg, hg), qm, kvm, vmt)


CMP_HALF = NSA_CMP_LEN // 2
N_CMP_ROWS = 128


def _compress_rows(xa, xb, w1_ref, w2_ref):
    half = CMP_HALF * HEAD_DIM
    y = _dot(xa, w1_ref[0, :half, :].astype(bf16))
    z = _dot(xb, w1_ref[0, half:, :].astype(bf16))
    parts = []
    for r in range(y.shape[0] // N_CMP_ROWS):
        zr = z[r * N_CMP_ROWS:(r + 1) * N_CMP_ROWS]
        parts.append(y[r * N_CMP_ROWS:(r + 1) * N_CMP_ROWS] + pltpu.roll(zr, N_CMP_ROWS - 1, 0))
    hid = _gelu(jnp.concatenate(parts, axis=0))
    return _dot(hid.astype(bf16), w2_ref[0].astype(bf16))


def _cmp_prompt_kernel(x0_ref, x1_ref, pe_ref, w1_ref, w2_ref, o_ref):
    pe = pe_ref[0]
    xa, xb = [], []
    for x_ref in (x0_ref, x1_ref):
        pa, pb = [], []
        for l in range(CMP_HALF):
            xl = x_ref[pl.ds(l, N_CMP_ROWS, stride=CMP_HALF), :]
            pa.append((xl + pe[l:l + 1]).astype(bf16))
            pb.append((xl + pe[CMP_HALF + l:CMP_HALF + l + 1]).astype(bf16))
        xa.append(jnp.concatenate(pa, axis=1))
        xb.append(jnp.concatenate(pb, axis=1))
    out = _compress_rows(jnp.concatenate(xa, axis=0), jnp.concatenate(xb, axis=0), w1_ref, w2_ref)
    for g in range(NSA_KV_HEADS):
        o_ref[0, 0, g] = out[g * N_CMP_ROWS:(g + 1) * N_CMP_ROWS]


def compress_prompt(kvc, pe2, w12, w22, batch, seq):
    return pl.pallas_call(
        _cmp_prompt_kernel,
        grid=(batch, 2),
        in_specs=[pl.BlockSpec((seq, HEAD_DIM), lambda b, kv: (b, NSA_KV_HEADS * kv)),
                  pl.BlockSpec((seq, HEAD_DIM), lambda b, kv: (b, NSA_KV_HEADS * kv + 1)),
                  pl.BlockSpec((1, NSA_CMP_LEN, HEAD_DIM), lambda b, kv: (kv, 0, 0)),
                  pl.BlockSpec((1, NSA_CMP_LEN * HEAD_DIM, NSA_CMP_HIDDEN), lambda b, kv: (kv, 0, 0)),
                  pl.BlockSpec((1, NSA_CMP_HIDDEN, HEAD_DIM), lambda b, kv: (kv, 0, 0))],
        out_specs=pl.BlockSpec((1, 1, NSA_KV_HEADS, N_CMP_ROWS, HEAD_DIM), lambda b, kv: (b, kv, 0, 0, 0)),
        out_shape=jax.ShapeDtypeStruct((batch, 2, NSA_KV_HEADS, N_CMP_ROWS, HEAD_DIM), f32),
        compiler_params=_cparams(("arbitrary", "arbitrary")),
        name="compress_prompt",
    )(kvc, kvc, pe2, w12, w22)


NSA_TQ = 128
N_CMP = 127


def _masked_softmax(s, mask):
    s = jnp.where(mask, s, NEG)
    m = jnp.max(s, axis=-1, keepdims=True)
    m = jnp.where(m > NEG, m, 0.0)
    p = jnp.where(mask, jnp.exp(s - m), 0.0)
    return p / jnp.maximum(jnp.sum(p, axis=-1, keepdims=True), 1e-30)


def _overlap_matrix():
    c = lax.broadcasted_iota(jnp.int32, (LANES, LANES), 0)
    j = lax.broadcasted_iota(jnp.int32, (LANES, LANES), 1)
    cs = NSA_CMP_STRIDE * c
    bs = NSA_SEL_BLOCK * j
    return ((cs < bs + NSA_SEL_BLOCK) & (cs + NSA_CMP_LEN - 1 >= bs)).astype(f32)


NSA_TK = 256
N_SEL_BLOCKS = 32


def _nsa_prompt_kernel(slope_ref, q_ref, gt_ref, ck_ref, cvt_ref, ks_ref, vst_ref, kw_ref, vwt_ref, o_ref, sel_ref):
    i = pl.program_id(1)
    tq, tk, R, G, ls, dh = NSA_TQ, NSA_TK, NSA_GROUP, NSA_KV_HEADS, NSA_SEL_BLOCK, HEAD_DIM
    W = R * tq
    q_all = q_ref[...]
    gates_t = gt_ref[...].T
    n_lane = lax.broadcasted_iota(jnp.int32, (1, W), 1) & (tq - 1)
    t_lane = i * tq + n_lane
    t_q = i * tq + lax.broadcasted_iota(jnp.int32, (N_SEL_BLOCKS, tq), 1)
    jrow = lax.broadcasted_iota(jnp.int32, (N_SEL_BLOCKS, tq), 0)
    crow = lax.broadcasted_iota(jnp.int32, (N_CMP_ROWS, W), 0)
    visible = (NSA_CMP_STRIDE * crow + NSA_CMP_LEN - 1 <= t_lane) & (crow < N_CMP)
    oj = lax.broadcasted_iota(jnp.int32, (N_SEL_BLOCKS, N_CMP_ROWS), 0) * ls
    oc = lax.broadcasted_iota(jnp.int32, (N_SEL_BLOCKS, N_CMP_ROWS), 1) * NSA_CMP_STRIDE
    overlap_t = ((oc < oj + ls) & (oc + NSA_CMP_LEN - 1 >= oj)).astype(f32)
    krow = lax.broadcasted_iota(jnp.int32, (tk, W), 0)
    rel_i = n_lane - krow
    rel = rel_i.astype(f32)
    diag = (i * tq) // tk
    diag_off = pl.multiple_of(diag * tk, tk)

    def chosen_rows(g, kt):
        parts = [jnp.broadcast_to(sel_ref[g, pl.ds(kt * (tk // ls) + b, 1), :], (ls, W)) for b in range(tk // ls)]
        return jnp.concatenate(parts, axis=0) > 0.5

    groups = []
    for g in range(G):
        gs = slice(g * dh, (g + 1) * dh)
        qs = jnp.concatenate([q_all[:, (g * R + r) * dh:(g * R + r + 1) * dh] for r in range(R)], axis=0)
        slope = slope_ref[g]
        s_c = jnp.where(visible, _dot_nt(ck_ref[0, g], qs, HIGHEST) * SCALE, NEG)
        m_c = jnp.max(s_c, axis=0, keepdims=True)
        p_c = jnp.where(visible, jnp.exp(s_c - jnp.where(m_c > NEG, m_c, 0.0)), 0.0)
        p_c = p_c / jnp.maximum(jnp.sum(p_c, axis=0, keepdims=True), 1e-30)
        o_c = _dot(cvt_ref[0, g].astype(bf16), p_c.astype(bf16))
        p_sum = p_c[:, 0:tq]
        for r in range(1, R):
            p_sum = p_sum + p_c[:, r * tq:(r + 1) * tq]
        imp = _dot(overlap_t, p_sum, HIGHEST)
        sel = _topk_rows(jnp.where(jrow < t_q // ls, imp, NEG), NSA_SEL_TOPK)
        sel_ref[g] = jnp.concatenate([sel] * R, axis=1)
        qb = (qs * SCALE).astype(bf16)
        bias = slope * rel
        d0 = slope * (i * tq - diag * tk).astype(f32)
        dist = rel_i + (i * tq - diag * tk)
        key_blk = (diag * tk + krow) // ls
        s = _dot_nt(ks_ref[pl.ds(diag_off, tk), gs].astype(bf16), qb) - bias - d0
        ok = chosen_rows(g, diag) | ((key_blk == t_lane // ls) & (dist >= 0))
        slc0 = _softmax_first(jnp.where(ok, s, NEG), vst_ref[gs, pl.ds(diag_off, tk)].astype(bf16))
        s = _dot_nt(kw_ref[pl.ds(diag_off, tk), gs].astype(bf16), qb) - bias - d0
        win0 = _softmax_first(jnp.where((dist >= 0) & (dist < NSA_WINDOW), s, NEG),
                              vwt_ref[gs, pl.ds(diag_off, tk)].astype(bf16))
        groups.append((gs, qb, bias, slope, o_c, slc0, win0))

    def slc_body(kt, carry):
        off = pl.multiple_of(kt * tk, tk)
        out = []
        for g, (gs, qb, bias, slope, *_rest) in enumerate(groups):
            s = _dot_nt(ks_ref[pl.ds(off, tk), gs].astype(bf16), qb) - bias - slope * (i * tq - kt * tk).astype(f32)
            s = jnp.where(chosen_rows(g, kt), s, NEG)
            out.append(_softmax_next(s, vst_ref[gs, pl.ds(off, tk)].astype(bf16), *carry[g]))
        return tuple(out)

    def win_body(kt, carry):
        off = pl.multiple_of(kt * tk, tk)
        shift = i * tq - kt * tk
        out = []
        for g, (gs, qb, bias, slope, *_rest) in enumerate(groups):
            s = _dot_nt(kw_ref[pl.ds(off, tk), gs].astype(bf16), qb) - bias - slope * shift.astype(f32)
            s = jnp.where(rel_i + shift < NSA_WINDOW, s, NEG)
            out.append(_softmax_next(s, vwt_ref[gs, pl.ds(off, tk)].astype(bf16), *carry[g]))
        return tuple(out)

    slc = lax.fori_loop(0, diag, slc_body, tuple(grp[5] for grp in groups))
    first_win = jnp.maximum(i * tq - (NSA_WINDOW - 1), 0) // tk
    win = lax.fori_loop(first_win, diag, win_body, tuple(grp[6] for grp in groups))

    for g, (gs, qb, bias, slope, o_c, *_rest) in enumerate(groups):
        def gate_row(branch):
            return jnp.concatenate([gates_t[NSA_BRANCHES * (g * R + r) + branch:NSA_BRANCHES * (g * R + r) + branch + 1]
                                    for r in range(R)], axis=1)
        o_t = (gate_row(0) * o_c + gate_row(1) * (slc[g][2] / slc[g][1]) + gate_row(2) * (win[g][2] / win[g][1]))
        o = o_t.T
        for r in range(R):
            o_ref[:, (g * R + r) * dh:(g * R + r + 1) * dh] = o[r * tq:(r + 1) * tq]


def _alibi_groups(n_groups, group):
    n = n_groups * group
    s = (2.0 ** (-8.0 * np.arange(1, n + 1) / n)).reshape(n_groups, group)
    return jnp.asarray(np.broadcast_to(s[:, :, None], (n_groups, group, LANES)), dtype=f32)


def nsa_prompt(qn, gates, ck, cvt, kvs, vst, kvw, vwt, batch, seq):
    assert seq // NSA_SEL_BLOCK == N_SEL_BLOCKS and (seq - NSA_CMP_LEN) // NSA_CMP_STRIDE + 1 == N_CMP
    nq = seq // NSA_TQ
    G, R = NSA_KV_HEADS, NSA_GROUP
    n = G * R
    slopes = (2.0 ** (-8.0 * np.arange(1, n + 1) / n)).reshape(G, 1, R, 1)
    slope_lanes = jnp.asarray(np.broadcast_to(slopes, (G, 1, R, NSA_TQ)).reshape(G, 1, R * NSA_TQ), dtype=f32)
    full = lambda a: pl.BlockSpec(a.shape, lambda b, i: (0,) * a.ndim)
    return pl.pallas_call(
        _nsa_prompt_kernel,
        grid=(batch, nq),
        in_specs=[full(slope_lanes),
                  pl.BlockSpec((NSA_TQ, NSA_W), lambda b, i: (b * nq + i, 0)),
                  pl.BlockSpec((NSA_TQ, LANES), lambda b, i: (b * nq + i, 0)),
                  pl.BlockSpec((1, G, N_CMP_ROWS, HEAD_DIM), lambda b, i: (b, 0, 0, 0)),
                  pl.BlockSpec((1, G, HEAD_DIM, N_CMP_ROWS), lambda b, i: (b, 0, 0, 0)),
                  pl.BlockSpec((seq, NSA_KVW), lambda b, i: (b, 0)),
                  pl.BlockSpec((NSA_KVW, seq), lambda b, i: (0, b)),
                  pl.BlockSpec((seq, NSA_KVW), lambda b, i: (b, 0)),
                  pl.BlockSpec((NSA_KVW, seq), lambda b, i: (0, b))],
        out_specs=pl.BlockSpec((NSA_TQ, NSA_W), lambda b, i: (b * nq + i, 0)),
        out_shape=jax.ShapeDtypeStruct((batch * seq, NSA_W), f32),
        scratch_shapes=[pltpu.VMEM((G, N_SEL_BLOCKS, R * NSA_TQ), f32)],
        compiler_params=_cparams(("arbitrary", "arbitrary")),
        name="nsa_prompt",
    )(slope_lanes, qn, gates, ck, cvt, kvs, vst, kvw, vwt)


def _head_slopes(n):
    s = 2.0 ** (-8.0 * np.arange(1, n + 1) / n)
    return jnp.asarray(np.broadcast_to(s[:, None], (n, LANES)), dtype=f32)


def _moba_sample_kernel(pt_ref, slope_ref, q_ref, kvn_ref, *refs):
    page_refs, o_ref = refs[:-1], refs[-1]
    n_pages = len(page_refs)
    mb = MOBA_BLOCK
    ppb = mb // PAGE_SIZE
    nb = n_pages // ppb
    t_new = n_pages * PAGE_SIZE
    q = q_ref[0]
    slope = slope_ref[...]
    tok = lax.broadcasted_iota(jnp.int32, (mb, MOBA_HEADS, LANES), 0)
    ones = jnp.ones((HEAD_DIM, LANES), bf16)
    gates, ms, ls, os_ = [], [], [], []
    for j in range(nb):
        k = jnp.concatenate([page_refs[ppb * j + u][pl.ds(0, PAGE_SIZE, stride=2)] for u in range(ppb)], axis=0)
        v = jnp.concatenate([page_refs[ppb * j + u][pl.ds(1, PAGE_SIZE, stride=2)] for u in range(ppb)], axis=0)
        kmean = jnp.sum(k, axis=0) / mb
        gates.append(jnp.sum(q * kmean, axis=-1, keepdims=True))
        dist = (t_new - j * mb - tok).astype(f32)
        kq = (k * q[None]).reshape(mb * MOBA_HEADS, HEAD_DIM).astype(bf16)
        s = _dot(kq, ones).reshape(mb, MOBA_HEADS, LANES) * SCALE - slope[None] * dist
        m = jnp.max(s, axis=0)
        p = jnp.exp(s - m[None])
        ms.append(m)
        ls.append(jnp.sum(p, axis=0))
        os_.append(jnp.sum(p * v, axis=0))
    chosen = []
    for j in range(nb):
        rank = jnp.zeros_like(gates[j])
        for j2 in range(nb):
            if j2 != j:
                ahead = (gates[j2] >= gates[j]) if j2 < j else (gates[j2] > gates[j])
                rank = rank + ahead.astype(f32)
        chosen.append(rank < MOBA_TOPK)
    kn, vn = kvn_ref[0, 0], kvn_ref[0, 1]
    s_own = jnp.sum(q * kn, axis=-1, keepdims=True) * SCALE
    m_all = s_own
    for j in range(nb):
        m_all = jnp.maximum(m_all, jnp.where(chosen[j], ms[j], NEG))
    w_own = jnp.exp(s_own - m_all)
    l_all = w_own
    o_all = w_own * vn
    for j in range(nb):
        w = jnp.where(chosen[j], jnp.exp(ms[j] - m_all), 0.0)
        l_all = l_all + w * ls[j]
        o_all = o_all + w * os_[j]
    o_ref[0] = o_all / l_all


def moba_sample(q3, kvn4, cache3, page_table):
    nseq, n_pages = page_table.shape
    rows = 2 * PAGE_SIZE
    page_spec = lambda p: pl.BlockSpec((rows, MOBA_HEADS, HEAD_DIM), lambda b, pt: (pt[b, p], 0, 0))
    return pl.pallas_call(
        _moba_sample_kernel,
        grid_spec=pltpu.PrefetchScalarGridSpec(
            num_scalar_prefetch=1, grid=(nseq,),
            in_specs=[pl.BlockSpec((MOBA_HEADS, LANES), lambda b, pt: (0, 0)),
                      pl.BlockSpec((1, MOBA_HEADS, HEAD_DIM), lambda b, pt: (b, 0, 0)),
                      pl.BlockSpec((1, 2, MOBA_HEADS, HEAD_DIM), lambda b, pt: (b, 0, 0, 0))]
                     + [page_spec(p) for p in range(n_pages)],
            out_specs=pl.BlockSpec((1, MOBA_HEADS, HEAD_DIM), lambda b, pt: (b, 0, 0))),
        out_shape=jax.ShapeDtypeStruct((nseq, MOBA_HEADS, HEAD_DIM), f32),
        compiler_params=_cparams(("arbitrary",)),
        name="moba_sample",
    )(page_table, _head_slopes(MOBA_HEADS), q3, kvn4, *([cache3] * n_pages))


KV_ROWS = 2 * NSA_KV_HEADS


def _nsa_sample_compress_kernel(pt_ref, pe_ref, w1k_ref, w1v_ref, w2k_ref, w2v_ref, *refs):
    page_refs, ckv_ref = refs[:-1], refs[-1]
    per_page = PAGE_SIZE // CMP_HALF
    G = NSA_KV_HEADS
    for kv, (w1_ref, w2_ref) in enumerate(((w1k_ref, w2k_ref), (w1v_ref, w2v_ref))):
        pe = pe_ref[kv]
        xa, xb = [], []
        for g in range(G):
            pa, pb = [], []
            for l in range(CMP_HALF):
                xl = jnp.concatenate(
                    [pr[pl.ds(KV_ROWS * l + G * kv + g, per_page, stride=KV_ROWS * CMP_HALF), :] for pr in page_refs], axis=0)
                pa.append((xl + pe[l:l + 1]).astype(bf16))
                pb.append((xl + pe[CMP_HALF + l:CMP_HALF + l + 1]).astype(bf16))
            xa.append(jnp.concatenate(pa, axis=1))
            xb.append(jnp.concatenate(pb, axis=1))
        comp = _compress_rows(jnp.concatenate(xa, axis=0), jnp.concatenate(xb, axis=0), w1_ref, w2_ref)
        for g in range(G):
            ckv_ref[0, kv, g] = comp[g * N_CMP_ROWS:(g + 1) * N_CMP_ROWS]


SELECT_BATCH = 8


def _nsa_sample_select_kernel(q_ref, ckv_ref, oc_ref, sel_ref, *, t_new):
    sb = q_ref.shape[0]
    G, R, H = NSA_KV_HEADS, NSA_GROUP, NSA_HEADS
    row = lax.broadcasted_iota(jnp.int32, (H, LANES), 0)
    lane = lax.broadcasted_iota(jnp.int32, (sb * H, LANES), 1)
    n_cmp = (t_new + 1 - NSA_CMP_LEN) // NSA_CMP_STRIDE + 1
    visible = (lane < n_cmp) & (NSA_CMP_STRIDE * lane + NSA_CMP_LEN - 1 <= t_new)
    first = row < R
    s_c = jnp.concatenate(
        [jnp.where(first, _dot_nt(q_ref[b], ckv_ref[b, 0, 0], HIGHEST), _dot_nt(q_ref[b], ckv_ref[b, 0, 1], HIGHEST))
         for b in range(sb)], axis=0) * SCALE
    p_c = _masked_softmax(s_c, visible)
    p_b = p_c.astype(bf16)
    for b in range(sb):
        pb = p_b[b * H:(b + 1) * H]
        oc_ref[b] = jnp.where(first, _dot(pb, ckv_ref[b, 1, 0].astype(bf16)), _dot(pb, ckv_ref[b, 1, 1].astype(bf16)))
    gi = lax.broadcasted_iota(jnp.int32, (sb * G, sb * H), 0)
    gj = lax.broadcasted_iota(jnp.int32, (sb * G, sb * H), 1)
    p_sum = _dot((gj // R == gi).astype(f32), p_c, HIGHEST)
    imp = _dot(p_sum, _overlap_matrix(), HIGHEST)
    lane_g = lax.broadcasted_iota(jnp.int32, (sb * G, LANES), 1)
    cur = jnp.where(lane_g < t_new // NSA_SEL_BLOCK, imp, NEG)
    out = jnp.zeros((sb * G, LANES), jnp.int32)
    for s in range(NSA_SEL_TOPK):
        m = jnp.max(cur, axis=-1, keepdims=True)
        idx = jnp.min(jnp.where((cur == m) & (m > NEG), lane_g, LANES), axis=-1, keepdims=True)
        found = idx < LANES
        out = jnp.where(lane_g == s, jnp.where(found, idx, 0), out)
        out = jnp.where(lane_g == NSA_SEL_TOPK + s, found.astype(jnp.int32), out)
        cur = jnp.where(lane_g == idx, NEG, cur)
    sel_ref[...] = out


def nsa_sample_cmp(q3, cache2, page_table, pe2, w1k, w1v, w2k, w2v):
    nseq, n_pages = page_table.shape
    G = NSA_KV_HEADS
    rows = PAGE_SIZE * KV_ROWS
    page_spec = lambda p: pl.BlockSpec((rows, HEAD_DIM), lambda b, pt: (pt[b, p], 0))
    full = lambda a: pl.BlockSpec(a.shape, lambda b, pt: (0,) * a.ndim)
    ckv = pl.pallas_call(
        _nsa_sample_compress_kernel,
        grid_spec=pltpu.PrefetchScalarGridSpec(
            num_scalar_prefetch=1, grid=(nseq,),
            in_specs=[full(pe2), full(w1k), full(w1v), full(w2k), full(w2v)]
                     + [page_spec(p) for p in range(n_pages)],
            out_specs=pl.BlockSpec((1, 2, G, N_CMP_ROWS, HEAD_DIM), lambda b, pt: (b, 0, 0, 0, 0))),
        out_shape=jax.ShapeDtypeStruct((nseq, 2, G, N_CMP_ROWS, HEAD_DIM), f32),
        compiler_params=_cparams(("arbitrary",)),
        name="nsa_sample_compress",
    )(page_table, pe2, w1k, w1v, w2k, w2v, *([cache2] * n_pages))
    sb = SELECT_BATCH
    return pl.pallas_call(
        functools.partial(_nsa_sample_select_kernel, t_new=n_pages * PAGE_SIZE),
        grid=(nseq // sb,),
        in_specs=[pl.BlockSpec((sb, NSA_HEADS, HEAD_DIM), lambda i: (i, 0, 0)),
                  pl.BlockSpec((sb, 2, G, N_CMP_ROWS, HEAD_DIM), lambda i: (i, 0, 0, 0, 0))],
        out_specs=[pl.BlockSpec((sb, NSA_HEADS, HEAD_DIM), lambda i: (i, 0, 0)),
                   pl.BlockSpec((sb * G, LANES), lambda i: (i, 0))],
        out_shape=[jax.ShapeDtypeStruct((nseq, NSA_HEADS, HEAD_DIM), f32),
                   jax.ShapeDtypeStruct((nseq * G, LANES), jnp.int32)],
        compiler_params=_cparams(("arbitrary",)),
        name="nsa_sample_select",
    )(q3, ckv)


def _decode_attend(q, slope, keys, vals, pos, valid, k_own, v_own, t_new):
    s = _dot_nt(q.astype(bf16), keys.astype(bf16)) * SCALE - slope * (t_new - pos).astype(f32)
    s = jnp.where(valid, s, NEG)
    s_own = jnp.sum(q * k_own, axis=-1, keepdims=True) * SCALE
    m = jnp.maximum(jnp.max(s, axis=-1, keepdims=True), s_own)
    p = jnp.exp(s - m)
    p_own = jnp.exp(s_own - m)
    denom = jnp.sum(p, axis=-1, keepdims=True) + p_own
    return (_dot(p.astype(bf16), vals.astype(bf16)) + p_own * v_own) / denom


def _nsa_sample_attn_kernel(pt_ref, sel_ref, slope_ref, q_ref, gt_ref, oc_ref, ksn_ref, kwn_ref, win_ref, *refs):
    blk_refs, (o_ref, wout_ref) = refs[:-2], refs[-2:]
    b = pl.program_id(0)
    G, R, ls, K = NSA_KV_HEADS, NSA_GROUP, NSA_SEL_BLOCK, NSA_SEL_TOPK
    t_new = pt_ref.shape[1] * PAGE_SIZE
    q = q_ref[0]
    slope = slope_ref[:, :1]
    row = lax.broadcasted_iota(jnp.int32, (G * R, HEAD_DIM), 0)
    own_rows = lambda ref, kv: jnp.where(row < R, ref[0, G * kv:G * kv + 1], ref[0, G * kv + 1:G * kv + 2])
    lane_s = lax.broadcasted_iota(jnp.int32, (1, K * ls), 1)
    n_win = win_ref.shape[0] // KV_ROWS
    lane_w = lax.broadcasted_iota(jnp.int32, (1, n_win), 1)
    pos_w = t_new - n_win + lane_w
    o_s = jnp.zeros((G * R, HEAD_DIM), f32)
    o_w = jnp.zeros((G * R, HEAD_DIM), f32)
    for g in range(G):
        keys = jnp.concatenate([blk_refs[g * K + s][pl.ds(g, ls, stride=KV_ROWS), :] for s in range(K)], axis=0)
        vals = jnp.concatenate([blk_refs[g * K + s][pl.ds(G + g, ls, stride=KV_ROWS), :] for s in range(K)], axis=0)
        pos = jnp.zeros((1, K * ls), jnp.int32)
        valid = jnp.zeros((1, K * ls), jnp.int32)
        for s in range(K):
            here = lane_s // ls == s
            pos = jnp.where(here, sel_ref[b, g * 2 * K + s] * ls + lane_s - s * ls, pos)
            valid = jnp.where(here, sel_ref[b, g * 2 * K + K + s], valid)
        mine = (row >= g * R) & (row < (g + 1) * R)
        o_s = jnp.where(mine, _decode_attend(q, slope, keys, vals, pos, valid > 0, own_rows(ksn_ref, 0), own_rows(ksn_ref, 1), t_new), o_s)
        keys_w = win_ref[pl.ds(g, n_win, stride=KV_ROWS), :]
        vals_w = win_ref[pl.ds(G + g, n_win, stride=KV_ROWS), :]
        ok_w = (pos_w > t_new - NSA_WINDOW) & (pos_w >= 0)
        o_w = jnp.where(mine, _decode_attend(q, slope, keys_w, vals_w, pos_w, ok_w, own_rows(kwn_ref, 0), own_rows(kwn_ref, 1), t_new), o_w)
    gt = gt_ref[0]
    o_ref[0] = gt[:, 0:1] * oc_ref[0] + gt[:, 1:2] * o_s + gt[:, 2:3] * o_w
    total = win_ref.shape[0]
    shifted = pltpu.roll(win_ref[...], total - KV_ROWS, 0)
    new8 = jnp.concatenate([kwn_ref[0], kwn_ref[0]], axis=0)
    row8 = lax.broadcasted_iota(jnp.int32, (2 * KV_ROWS, HEAD_DIM), 0)
    wout_ref[pl.ds(0, total - 2 * KV_ROWS), :] = shifted[:total - 2 * KV_ROWS]
    wout_ref[pl.ds(total - 2 * KV_ROWS, 2 * KV_ROWS), :] = jnp.where(row8 >= KV_ROWS, new8, shifted[total - 2 * KV_ROWS:])


def nsa_sample_attn(q3, gates3, o_c, ksn, kwn, slc2, win2, page_table, sel_flat):
    nseq = page_table.shape[0]
    G, K, ls = NSA_KV_HEADS, NSA_SEL_TOPK, NSA_SEL_BLOCK
    blocks_per_page = PAGE_SIZE // ls
    n_blocks = page_table.shape[1] * blocks_per_page
    win_rows = win2.shape[0] // nseq

    def blk_spec(g, s):
        def index(b, pt, sel):
            bb = jnp.minimum(b, nseq - 1)
            blk = jnp.clip(sel[bb, g * 2 * K + s], 0, n_blocks - 1)
            return (pt[bb, blk // blocks_per_page] * blocks_per_page + blk % blocks_per_page, 0)
        return pl.BlockSpec((ls * KV_ROWS, HEAD_DIM), index)

    per_seq = lambda shape: pl.BlockSpec((1,) + shape, lambda b, pt, sel: (b,) + (0,) * len(shape))
    return pl.pallas_call(
        _nsa_sample_attn_kernel,
        grid_spec=pltpu.PrefetchScalarGridSpec(
            num_scalar_prefetch=2, grid=(nseq,),
            in_specs=[pl.BlockSpec((NSA_HEADS, LANES), lambda b, pt, sel: (0, 0)),
                      per_seq((NSA_HEADS, HEAD_DIM)), per_seq((NSA_HEADS, LANES)), per_seq((NSA_HEADS, HEAD_DIM)),
                      per_seq((KV_ROWS, HEAD_DIM)), per_seq((KV_ROWS, HEAD_DIM)),
                      pl.BlockSpec((win_rows, HEAD_DIM), lambda b, pt, sel: (b, 0))]
                     + [blk_spec(g, s) for g in range(G) for s in range(K)],
            out_specs=[per_seq((NSA_HEADS, HEAD_DIM)),
                       pl.BlockSpec((win_rows, HEAD_DIM), lambda b, pt, sel: (b, 0))]),
        out_shape=[jax.ShapeDtypeStruct((nseq, NSA_HEADS, HEAD_DIM), f32),
                   jax.ShapeDtypeStruct(win2.shape, f32)],
        compiler_params=_cparams(("arbitrary",)),
        name="nsa_sample_attn",
    )(page_table, sel_flat, _head_slopes(NSA_HEADS), q3, gates3, o_c, ksn, kwn, win2, *([slc2] * (G * K)))


def _mid_kernel(x_ref, om_ref, on_ref, w_ref, g_ref, gate_ref, sh_ref, sc_ref, x1_ref, h2_ref):
    proj = (_dot(om_ref[...].astype(bf16), w_ref[:MOBA_W, :]) + _dot(on_ref[...].astype(bf16), w_ref[MOBA_W:, :]))
    x1 = x_ref[...] + gate_ref[0] * proj
    x1_ref[...] = x1
    h2_ref[...] = _rms_mod(x1, g_ref[...], sh_ref[0], sc_ref[0]).astype(bf16)


def mid_block(x, o_m, o_n, w_out_b, g_norm2, mods3, tm, rows_per_mod):
    t = x.shape[0]
    r = mods3.shape[1]
    tiles_per_mod = rows_per_mod // tm
    mod_spec = lambda which: pl.BlockSpec((1, r, D_MODEL), lambda i: (i // tiles_per_mod, 0, which))
    return pl.pallas_call(
        _mid_kernel,
        grid=(t // tm,),
        in_specs=[pl.BlockSpec((tm, D_MODEL), lambda i: (i, 0)),
                  pl.BlockSpec((tm, MOBA_W), lambda i: (i, 0)),
                  pl.BlockSpec((tm, NSA_W), lambda i: (i, 0)),
                  pl.BlockSpec((MOBA_W + NSA_W, D_MODEL), lambda i: (0, 0)),
                  pl.BlockSpec((1, D_MODEL), lambda i: (0, 0)),
                  mod_spec(2), mod_spec(3), mod_spec(4)],
        out_specs=[pl.BlockSpec((tm, D_MODEL), lambda i: (i, 0)),
                   pl.BlockSpec((tm, D_MODEL), lambda i: (i, 0))],
        out_shape=[jax.ShapeDtypeStruct((t, D_MODEL), f32), jax.ShapeDtypeStruct((t, D_MODEL), bf16)],
        compiler_params=_cparams(("arbitrary",)),
        name="mid_block",
    )(x, o_m, o_n, w_out_b, g_norm2.reshape(1, D_MODEL), mods3, mods3, mods3)


PEER_HALF = PEER_QDIM // 2
PEER_A_FULL = 8


def _top_values(s, k):
    tops = []
    cur = s
    for _ in range(k):
        m = jnp.max(cur, axis=0, keepdims=True)
        tops.append(m)
        cur = jnp.where(cur == m, NEG, cur)
    return jnp.concatenate(tops, axis=0)


def _peer_route_kernel(h_ref, wq_ref, keys_ref, s1_ref, thr_ref, e1_ref, coef_ref):
    q = _dot(h_ref[...], wq_ref[...])
    k = PEER_TOPK
    for h in range(PEER_HEADS):
        base = h * PEER_QDIM
        s0 = _dot_nt(keys_ref[h, 0], q[:, base:base + PEER_HALF], HIGHEST)
        s1 = _dot_nt(keys_ref[h, 1], q[:, base + PEER_HALF:base + PEER_QDIM], HIGHEST)
        top0 = _top_values(s0, k)
        top1 = _top_values(s1, k)
        cand = jnp.concatenate([top0[a:a + 1] + top1 for a in range(PEER_A_FULL)]
                               + [top0[PEER_A_FULL:] + top1[0:1]], axis=0)
        best = _top_values(cand, k)
        tau = best[k - 1:k]
        z = jnp.sum(jnp.exp(best - best[0:1]), axis=0, keepdims=True)
        thr = jnp.full(s0.shape, jnp.inf, f32)
        for a in range(k):
            thr_a = jnp.min(jnp.where(top0[a:a + 1] + top1 >= tau, top1, jnp.inf), axis=0, keepdims=True)
            thr = jnp.where(s0 == top0[a:a + 1], thr_a, thr)
        s1_ref[h] = s1
        thr_ref[h] = thr
        e1_ref[h] = jnp.exp(s1 - top1[0:1])
        coef_ref[h] = jnp.exp(s0 - top0[0:1]) / z


def peer_route(h2, wq_b, keys, tm):
    t = h2.shape[0]
    out_spec = pl.BlockSpec((PEER_HEADS, PEER_KEYS, tm), lambda i: (0, 0, i))
    out_shape = jax.ShapeDtypeStruct((PEER_HEADS, PEER_KEYS, t), f32)
    return pl.pallas_call(
        _peer_route_kernel,
        grid=(t // tm,),
        in_specs=[pl.BlockSpec((tm, D_MODEL), lambda i: (i, 0)),
                  pl.BlockSpec(wq_b.shape, lambda i: (0, 0)),
                  pl.BlockSpec(keys.shape, lambda i: (0, 0, 0, 0))],
        out_specs=[out_spec] * 4,
        out_shape=[out_shape] * 4,
        compiler_params=_cparams(("arbitrary",)),
        name="peer_route",
    )(h2, wq_b, keys)


PEER_EXPERT_TILE = 1024
PEER_SLAB = 512


def _peer_expert_kernel(ht_ref, u_ref, vt_ref, s1_ref, thr_ref, e1_ref, coef_ref, x1_ref, gate_ref, gf_ref,
                        y_ref, acc_ref, p_ref):
    e = pl.program_id(1)
    te = u_ref.shape[0]
    rows_per_step = te // PEER_KEYS

    @pl.when(e == 0)
    def _():
        acc_ref[...] = jnp.zeros_like(acc_ref)

    tm = ht_ref.shape[1]
    for slab in range(te // PEER_SLAB):
        act = _gelu(_dot(u_ref[slab * PEER_SLAB:(slab + 1) * PEER_SLAB, :], ht_ref[...]))
        for rr in range(PEER_SLAB // PEER_KEYS):
            r = slab * (PEER_SLAB // PEER_KEYS) + rr
            i0 = e * rows_per_step + r
            for c in range(tm // LANES):
                cs = slice(c * LANES, (c + 1) * LANES)
                w = jnp.zeros((PEER_KEYS, LANES), f32)
                for h in range(PEER_HEADS):
                    thr = thr_ref[h, pl.ds(i0, 1), :][:, cs]
                    coef = coef_ref[h, pl.ds(i0, 1), :][:, cs]
                    w = w + jnp.where(s1_ref[h, :, cs] >= thr, e1_ref[h, :, cs] * coef, 0.0)
                p_ref[r * PEER_KEYS:(r + 1) * PEER_KEYS, cs] = (
                    w * act[rr * PEER_KEYS:(rr + 1) * PEER_KEYS, cs]).astype(bf16)
    acc_ref[...] += _dot(vt_ref[...], p_ref[...])

    @pl.when(e == pl.num_programs(1) - 1)
    def _():
        y = x1_ref[...] + gate_ref[0] * acc_ref[...].T
        y_ref[...] = (y * lax.rsqrt(jnp.mean(y * y, axis=-1, keepdims=True) + RMS_EPS)) * gf_ref[...]


def peer_experts(h2t, u_b, vt_b, route, x1, mods3, g_final, tm, te, rows_per_mod):
    t = h2t.shape[1]
    r = mods3.shape[1]
    n_exp = u_b.shape[0]
    tiles_per_mod = rows_per_mod // tm
    once = dict(pipeline_mode=pl.Buffered(1))
    route_spec = pl.BlockSpec((PEER_HEADS, PEER_KEYS, tm), lambda i, e: (0, 0, i), **once)
    return pl.pallas_call(
        _peer_expert_kernel,
        grid=(t // tm, n_exp // te),
        in_specs=[pl.BlockSpec((D_MODEL, tm), lambda i, e: (0, i), **once),
                  pl.BlockSpec((te, D_MODEL), lambda i, e: (e, 0)),
                  pl.BlockSpec((D_MODEL, te), lambda i, e: (0, e)),
                  route_spec, route_spec, route_spec, route_spec,
                  pl.BlockSpec((tm, D_MODEL), lambda i, e: (i, 0), **once),
                  pl.BlockSpec((1, r, D_MODEL), lambda i, e: (i // tiles_per_mod, 0, 5)),
                  pl.BlockSpec((1, D_MODEL), lambda i, e: (0, 0))],
        out_specs=pl.BlockSpec((tm, D_MODEL), lambda i, e: (i, 0)),
        out_shape=jax.ShapeDtypeStruct((t, D_MODEL), f32),
        scratch_shapes=[pltpu.VMEM((D_MODEL, tm), f32), pltpu.VMEM((te, tm), bf16)],
        compiler_params=_cparams(("arbitrary", "arbitrary")),
        name="peer_experts",
    )(h2t, u_b, vt_b, *route, x1, mods3, g_final.reshape(1, D_MODEL))


def _group_forward(x2, mods3, tm, rows_per_mod, w, attend):
    proj = in_projection(x2, mods3, w["g_norm1"], w["w_in"], tm, rows_per_mod)
    o_m, o_n = attend(proj)
    x1, h2 = mid_block(x2, o_m, o_n, w["w_out"], w["g_norm2"], mods3, min(tm, 256), rows_per_mod)
    route = peer_route(h2, w["peer_w_q"], w["peer_keys"], min(tm, 256))
    y = peer_experts(h2.T, w["peer_u"], w["peer_vt"], route, x1, mods3, w["g_final"], tm, PEER_EXPERT_TILE, rows_per_mod)
    return proj, y


def kernel(x_prompt, x_sample, cache_moba_kv, cache_nsa_cmp_kv, cache_nsa_slc_kv, state_nsa_win_kv, page_table,
           c_prompt, c_sample, w_ada, b_ada, g_norm1, w_in, cmp_pe_k, cmp_w1_k, cmp_w2_k, cmp_pe_v, cmp_w1_v,
           cmp_w2_v, w_out, g_norm2, peer_w_q, peer_keys, peer_u, peer_v, g_final):
    assert w_ada.shape[0] == 1, "single layer"
    batch, seq, _ = x_prompt.shape
    nseq, dec_seq, _ = x_sample.shape
    assert dec_seq == 1 and state_nsa_win_kv.shape[2] == NSA_WINDOW and seq >= NSA_WINDOW
    G, H, dh = NSA_KV_HEADS, MOBA_HEADS, HEAD_DIM

    c_all = jnp.concatenate([c_prompt, c_sample], axis=0)
    pad = (-c_all.shape[0]) % 8
    mods = ada_mods(jnp.pad(c_all, ((0, pad), (0, 0))), w_ada[0], b_ada[0])
    mods_p = mods[:batch].reshape(batch, 1, N_MOD * D_MODEL)
    mods_s = mods[batch:batch + nseq].reshape(1, nseq, N_MOD * D_MODEL)

    p_in = w_in.shape[2]
    pe2 = jnp.stack([cmp_pe_k[0], cmp_pe_v[0]])
    w12 = jnp.stack([cmp_w1_k[0], cmp_w1_v[0]]).astype(bf16)
    w22 = jnp.stack([cmp_w2_k[0], cmp_w2_v[0]]).astype(bf16)
    w = dict(
        g_norm1=g_norm1[0], g_norm2=g_norm2[0], g_final=g_final,
        w_in=jnp.pad(w_in[0], ((0, 0), (0, IN_COLS - p_in))).astype(bf16),
        w_out=w_out[0].astype(bf16),
        peer_w_q=peer_w_q[0].astype(bf16), peer_keys=peer_keys[0],
        peer_u=peer_u[0].astype(bf16), peer_vt=peer_v[0].T.astype(bf16),
    )

    def attend_prompt(proj):
        qm, kvm, qn, kvc, kvs, kvw, gates = proj[:7]
        ckv = compress_prompt(kvc, pe2, w12, w22, batch, seq)
        o_m = moba_prompt(qm, kvm, kvm[:, MOBA_W:].T, batch, seq)
        o_n = nsa_prompt(qn, gates, ckv[:, 0], jnp.swapaxes(ckv[:, 1], 2, 3),
                         kvs, kvs[:, NSA_KVW:].T, kvw, kvw[:, NSA_KVW:].T, batch, seq)
        return o_m, o_n

    win_out = []

    def attend_sample(proj):
        qm, kvm, qn, kvc, kvs, kvw, gates = proj[:7]
        o_m = moba_sample(qm.reshape(nseq, H, dh), kvm.reshape(nseq, 2, H, dh), cache_moba_kv.reshape(-1, H, dh), page_table)
        qn3 = qn.reshape(nseq, NSA_HEADS, dh)
        o_c, sel = nsa_sample_cmp(qn3, cache_nsa_cmp_kv.reshape(-1, dh), page_table, pe2,
                                  w12[0:1], w12[1:2], w22[0:1], w22[1:2])
        sel_flat = sel[:, :2 * NSA_SEL_TOPK].reshape(nseq, G * 2 * NSA_SEL_TOPK)
        gates3 = jnp.pad(gates[:, :NSA_HEADS * NSA_BRANCHES].reshape(nseq, NSA_HEADS, NSA_BRANCHES),
                         ((0, 0), (0, 0), (0, LANES - NSA_BRANCHES)))
        o_n, wout = nsa_sample_attn(qn3, gates3, o_c, kvs.reshape(nseq, KV_ROWS, dh), kvw.reshape(nseq, KV_ROWS, dh),
                                    cache_nsa_slc_kv.reshape(-1, dh), state_nsa_win_kv.reshape(-1, dh), page_table, sel_flat)
        win_out.append(wout)
        return o_m.reshape(nseq, MOBA_W), o_n.reshape(nseq, NSA_W)

    proj_p, y_p = _group_forward(x_prompt.reshape(batch * seq, D_MODEL), mods_p, 512, seq, w, attend_prompt)
    proj_s, y_s = _group_forward(x_sample.reshape(nseq, D_MODEL), mods_s, nseq, nseq, w, attend_sample)

    kv_p = lambda a, nh: a.reshape(1, batch, seq, 2, nh, dh)
    kv_s = lambda a, nh: a.reshape(1, nseq, 1, 2, nh, dh)
    return (y_p.reshape(batch, seq, D_MODEL), y_s.reshape(nseq, 1, D_MODEL),
            kv_p(proj_p[7], H), kv_s(proj_s[7], H),
            kv_p(proj_p[8], G), kv_s(proj_s[8], G),
            kv_p(proj_p[9], G), kv_s(proj_s[9], G),
            kv_p(proj_p[10], G)[:, :, seq - NSA_WINDOW:],
            win_out[0].reshape(state_nsa_win_kv.shape))
```

```python
import functools

import jax
import jax.numpy as jnp
import numpy as np
from jax import lax
from jax.experimental import pallas as pl
from jax.experimental.pallas import tpu as pltpu

f32 = jnp.float32
bf16 = jnp.bfloat16
HIGHEST = lax.Precision.HIGHEST

D_MODEL = 2048
HEAD_DIM = 128
MOBA_HEADS = 8
NSA_HEADS = 8
NSA_KV_HEADS = 2
NSA_GROUP = 4
MOBA_W = MOBA_HEADS * HEAD_DIM
NSA_W = NSA_HEADS * HEAD_DIM
NSA_KVW = NSA_KV_HEADS * HEAD_DIM
MOBA_BLOCK = 256
MOBA_TOPK = 3
NSA_CMP_LEN = 32
NSA_CMP_STRIDE = 16
NSA_CMP_HIDDEN = 256
NSA_SEL_BLOCK = 64
NSA_SEL_TOPK = 4
NSA_WINDOW = 512
NSA_BRANCHES = 3
PEER_KEYS = 128
PEER_HEADS = 8
PEER_QDIM = 256
PEER_TOPK = 16
N_MOD = 6
RMS_EPS = 1e-6
PAGE_SIZE = 128
SCALE = HEAD_DIM ** -0.5
NEG = -jnp.inf
LANES = 128

IN_TILE = 512
IN_WIDE_TILES = 11
IN_COLS = IN_WIDE_TILES * IN_TILE + LANES
VMEM_LIMIT = 56 * 1024 * 1024


def _cparams(sem):
    return pltpu.CompilerParams(dimension_semantics=sem, vmem_limit_bytes=VMEM_LIMIT)


def _gelu(x):
    z2 = np.float32(2.0 * np.sqrt(2.0 / np.pi)) * (x + 0.044715 * (x * x * x))
    return x / (1.0 + jnp.exp(-z2))


def _dot_nt(a, b, precision=None):
    return lax.dot_general(a, b, (((1,), (1,)), ((), ())), precision=precision, preferred_element_type=f32)


def _dot(a, b, precision=None):
    return jnp.dot(a, b, precision=precision, preferred_element_type=f32)


def _ada_kernel(c_ref, w_ref, b_ref, o_ref):
    c = c_ref[...]
    a = c * jax.nn.sigmoid(c)
    o_ref[...] = _dot(a, w_ref[...], HIGHEST) + b_ref[...]


def ada_mods(c, w_ada, b_ada):
    rows = c.shape[0]
    n = w_ada.shape[1]
    tn = 1024
    return pl.pallas_call(
        _ada_kernel,
        grid=(n // tn,),
        in_specs=[pl.BlockSpec((rows, D_MODEL), lambda j: (0, 0)),
                  pl.BlockSpec((D_MODEL, tn), lambda j: (0, j)),
                  pl.BlockSpec((1, tn), lambda j: (0, j))],
        out_specs=pl.BlockSpec((rows, tn), lambda j: (0, j)),
        out_shape=jax.ShapeDtypeStruct((rows, n), f32),
        compiler_params=_cparams(("arbitrary",)),
        name="ada_mods",
    )(c, w_ada, b_ada.reshape(1, n))


def _rms_mod(x, g, shift, scale):
    y = x * lax.rsqrt(jnp.mean(x * x, axis=-1, keepdims=True) + RMS_EPS)
    return (y * g) * (1.0 + scale) + shift


def _inproj_kernel(x_ref, g_ref, sh_ref, sc_ref, w_ref, wg_ref,
                   qm_ref, kvm_ref, qn_ref, kvc_ref, kvs_ref, kvw_ref, gt_ref,
                   kvm_out_ref, kvc_out_ref, kvs_out_ref, kvw_out_ref, h_scr):
    j = pl.program_id(1)
    tm = x_ref.shape[0]
    heads_per_tile = IN_TILE // HEAD_DIM

    @pl.when(j == 0)
    def _():
        h = _rms_mod(x_ref[...], g_ref[...], sh_ref[0], sc_ref[0]).astype(bf16)
        h_scr[...] = h
        gt_ref[...] = jax.nn.sigmoid(_dot(h, wg_ref[...]))

    acc = _dot(h_scr[...], w_ref[...])

    @pl.when(j < 2)
    def _():
        qm_ref[...] = acc

    for p in range(2 * MOBA_W // IN_TILE):
        @pl.when(j == 2 + p)
        def _(p=p):
            kvm_ref[...] = acc
            kvm_out_ref[:, heads_per_tile * p:heads_per_tile * (p + 1), :] = acc.reshape(tm, heads_per_tile, HEAD_DIM)

    @pl.when((j >= 6) & (j < 8))
    def _():
        qn_ref[...] = acc

    for p, (flat_ref, out_ref) in enumerate(((kvc_ref, kvc_out_ref), (kvs_ref, kvs_out_ref), (kvw_ref, kvw_out_ref))):
        @pl.when(j == 8 + p)
        def _(flat_ref=flat_ref, out_ref=out_ref):
            flat_ref[...] = acc
            out_ref[...] = acc.reshape(tm, 2, NSA_KV_HEADS, HEAD_DIM)


def in_projection(x, mods3, g_norm1, w_in_p, tm, rows_per_mod):
    t = x.shape[0]
    r = mods3.shape[1]
    tiles_per_mod = rows_per_mod // tm
    mod_spec = lambda which: pl.BlockSpec((1, r, D_MODEL), lambda i, j: (i // tiles_per_mod, 0, which))
    clip = lambda j, lo, n: jnp.clip(j - lo, 0, n - 1)
    out_shapes = [jax.ShapeDtypeStruct((t, w), f32) for w in (MOBA_W, 2 * MOBA_W, NSA_W, 2 * NSA_KVW, 2 * NSA_KVW, 2 * NSA_KVW, LANES)]
    out_specs = [
        pl.BlockSpec((tm, IN_TILE), lambda i, j: (i, clip(j, 0, 2))),
        pl.BlockSpec((tm, IN_TILE), lambda i, j: (i, clip(j, 2, 4))),
        pl.BlockSpec((tm, IN_TILE), lambda i, j: (i, clip(j, 6, 2))),
        pl.BlockSpec((tm, IN_TILE), lambda i, j: (i, 0)),
        pl.BlockSpec((tm, IN_TILE), lambda i, j: (i, 0)),
        pl.BlockSpec((tm, IN_TILE), lambda i, j: (i, 0)),
        pl.BlockSpec((tm, LANES), lambda i, j: (i, 0)),
        pl.BlockSpec((tm, 2 * MOBA_HEADS, HEAD_DIM), lambda i, j: (i, 0, 0)),
    ] + [pl.BlockSpec((tm, 2, NSA_KV_HEADS, HEAD_DIM), lambda i, j: (i, 0, 0, 0))] * 3
    out_shapes += [jax.ShapeDtypeStruct((t, 2 * MOBA_HEADS, HEAD_DIM), f32)]
    out_shapes += [jax.ShapeDtypeStruct((t, 2, NSA_KV_HEADS, HEAD_DIM), f32)] * 3
    return pl.pallas_call(
        _inproj_kernel,
        grid=(t // tm, IN_WIDE_TILES),
        in_specs=[pl.BlockSpec((tm, D_MODEL), lambda i, j: (i, 0)),
                  pl.BlockSpec((1, D_MODEL), lambda i, j: (0, 0)),
                  mod_spec(0), mod_spec(1),
                  pl.BlockSpec((D_MODEL, IN_TILE), lambda i, j: (0, j)),
                  pl.BlockSpec((D_MODEL, LANES), lambda i, j: (0, IN_WIDE_TILES * IN_TILE // LANES))],
        out_specs=out_specs,
        out_shape=out_shapes,
        scratch_shapes=[pltpu.VMEM((tm, D_MODEL), bf16)],
        compiler_params=_cparams(("arbitrary", "arbitrary")),
        name="in_projection",
    )(x, g_norm1.reshape(1, D_MODEL), mods3, mods3, w_in_p, w_in_p)


def _topk_rows(score, k):
    rows = score.shape[0]
    sub = lax.broadcasted_iota(jnp.int32, score.shape, 0)
    sel = jnp.zeros(score.shape, f32)
    g = score
    for _ in range(k):
        m = jnp.max(g, axis=0, keepdims=True)
        hit = (g == m) & (m > NEG)
        idx = jnp.min(jnp.where(hit, sub, rows), axis=0, keepdims=True)
        pick = sub == idx
        sel = jnp.where(pick, 1.0, sel)
        g = jnp.where(pick, NEG, g)
    return sel


def _softmax_first(s, vt):
    m = jnp.max(s, axis=0, keepdims=True)
    p = jnp.exp(s - m)
    return m, jnp.sum(p, axis=0, keepdims=True), _dot(vt, p.astype(bf16))


def _softmax_next(s, vt, m_i, l_i, acc):
    m_new = jnp.maximum(m_i, jnp.max(s, axis=0, keepdims=True))
    alpha = jnp.exp(m_i - m_new)
    p = jnp.exp(s - m_new)
    return m_new, alpha * l_i + jnp.sum(p, axis=0, keepdims=True), alpha * acc + _dot(vt, p.astype(bf16))


MOBA_HEADS_PER_STEP = 4


def _moba_prompt_kernel(slope_ref, q_ref, k_ref, vt_ref, o_ref, sel_ref):
    i = pl.program_id(2)
    mb, dh, hg = MOBA_BLOCK, HEAD_DIM, MOBA_HEADS_PER_STEP
    nb = k_ref.shape[0] // mb
    krow = lax.broadcasted_iota(jnp.int32, (mb, mb), 0)
    qcol = lax.broadcasted_iota(jnp.int32, (mb, mb), 1)
    rel = (qcol - krow).astype(f32)
    blk = lax.broadcasted_iota(jnp.int32, (nb, mb), 0)
    own = pl.multiple_of(i * mb, mb)
    heads = []
    for hh in range(hg):
        cs = slice(hh * dh, (hh + 1) * dh)
        q = q_ref[:, cs]
        slope = slope_ref[0, hh:hh + 1, :1]
        kmean = jnp.concatenate(
            [jnp.mean(k_ref[pl.ds(j * mb, mb), cs], axis=0, keepdims=True) for j in range(nb)], axis=0)
        gate = _dot_nt(kmean, q, HIGHEST)
        sel_ref[hh] = _topk_rows(jnp.where(blk < i, gate, NEG), MOBA_TOPK)
        qb = (q * SCALE).astype(bf16)
        bias = slope * rel
        s = _dot_nt(k_ref[pl.ds(own, mb), cs].astype(bf16), qb) - bias
        s = jnp.where(krow <= qcol, s, NEG)
        heads.append((cs, qb, bias, slope, _softmax_first(s, vt_ref[cs, pl.ds(own, mb)].astype(bf16))))

    def body(j, carry):
        off = pl.multiple_of(j * mb, mb)
        out = []
        for hh, (cs, qb, bias, slope, _) in enumerate(heads):
            s = _dot_nt(k_ref[pl.ds(off, mb), cs].astype(bf16), qb) - bias - slope * ((i - j) * mb).astype(f32)
            s = jnp.where(sel_ref[hh, pl.ds(j, 1), :] > 0.5, s, NEG)
            out.append(_softmax_next(s, vt_ref[cs, pl.ds(off, mb)].astype(bf16), *carry[hh]))
        return tuple(out)

    final = lax.fori_loop(0, i, body, tuple(h[4] for h in heads))
    for hh, (cs, *_rest) in enumerate(heads):
        _, l_i, acc = final[hh]
        o_ref[:, cs] = (acc / l_i).T


def moba_prompt(qm, kvm, vmt, batch, seq):
    nq = seq // MOBA_BLOCK
    hg = MOBA_HEADS_PER_STEP
    wide = hg * HEAD_DIM
    return pl.pallas_call(
        _moba_prompt_kernel,
        grid=(batch, MOBA_HEADS // hg, nq),
        in_specs=[pl.BlockSpec((1, hg, LANES), lambda b, h, i: (h, 0, 0)),
                  pl.BlockSpec((MOBA_BLOCK, wide), lambda b, h, i: (b * nq + i, h)),
                  pl.BlockSpec((seq, wide), lambda b, h, i: (b, h)),
                  pl.BlockSpec((wide, seq), lambda b, h, i: (h, b))],
        out_specs=pl.BlockSpec((MOBA_BLOCK, wide), lambda b, h, i: (b * nq + i, h)),
        out_shape=jax.ShapeDtypeStruct((batch * seq, MOBA_W), f32),
        scratch_shapes=[pltpu.VMEM((hg, seq // MOBA_BLOCK, MOBA_BLOCK), f32)],
        compiler_params=_cparams(("arbitrary", "arbitrary", "arbitrary")),
        name="moba_prompt",
    )(_alibi_groups(MOBA_HEADS // hg, hg), qm, kvm, vmt)


CMP_HALF = NSA_CMP_LEN // 2
N_CMP_ROWS = 128


def _compress_rows(xa, xb, w1_ref, w2_ref):
    half = CMP_HALF * HEAD_DIM
    y = _dot(xa, w1_ref[0, :half, :].astype(bf16))
    z = _dot(xb, w1_ref[0, half:, :].astype(bf16))
    parts = []
    for r in range(y.shape[0] // N_CMP_ROWS):
        zr = z[r * N_CMP_ROWS:(r + 1) * N_CMP_ROWS]
        parts.append(y[r * N_CMP_ROWS:(r + 1) * N_CMP_ROWS] + pltpu.roll(zr, N_CMP_ROWS - 1, 0))
    hid = _gelu(jnp.concatenate(parts, axis=0))
    return _dot(hid.astype(bf16), w2_ref[0].astype(bf16))


def _cmp_prompt_kernel(x0_ref, x1_ref, pe_ref, w1_ref, w2_ref, o_ref):
    pe = pe_ref[0]
    xa, xb = [], []
    for x_ref in (x0_ref, x1_ref):
        pa, pb = [], []
        for l in range(CMP_HALF):
            xl = x_ref[pl.ds(l, N_CMP_ROWS, stride=CMP_HALF), :]
            pa.append((xl + pe[l:l + 1]).astype(bf16))
            pb.append((xl + pe[CMP_HALF + l:CMP_HALF + l + 1]).astype(bf16))
        xa.append(jnp.concatenate(pa, axis=1))
        xb.append(jnp.concatenate(pb, axis=1))
    out = _compress_rows(jnp.concatenate(xa, axis=0), jnp.concatenate(xb, axis=0), w1_ref, w2_ref)
    for g in range(NSA_KV_HEADS):
        o_ref[0, 0, g] = out[g * N_CMP_ROWS:(g + 1) * N_CMP_ROWS]


def compress_prompt(kvc, pe2, w12, w22, batch, seq):
    return pl.pallas_call(
        _cmp_prompt_kernel,
        grid=(batch, 2),
        in_specs=[pl.BlockSpec((seq, HEAD_DIM), lambda b, kv: (b, NSA_KV_HEADS * kv)),
                  pl.BlockSpec((seq, HEAD_DIM), lambda b, kv: (b, NSA_KV_HEADS * kv + 1)),
                  pl.BlockSpec((1, NSA_CMP_LEN, HEAD_DIM), lambda b, kv: (kv, 0, 0)),
                  pl.BlockSpec((1, NSA_CMP_LEN * HEAD_DIM, NSA_CMP_HIDDEN), lambda b, kv: (kv, 0, 0)),
                  pl.BlockSpec((1, NSA_CMP_HIDDEN, HEAD_DIM), lambda b, kv: (kv, 0, 0))],
        out_specs=pl.BlockSpec((1, 1, NSA_KV_HEADS, N_CMP_ROWS, HEAD_DIM), lambda b, kv: (b, kv, 0, 0, 0)),
        out_shape=jax.ShapeDtypeStruct((batch, 2, NSA_KV_HEADS, N_CMP_ROWS, HEAD_DIM), f32),
        compiler_params=_cparams(("arbitrary", "arbitrary")),
        name="compress_prompt",
    )(kvc, kvc, pe2, w12, w22)


NSA_TQ = 128
N_CMP = 127


def _masked_softmax(s, mask):
    s = jnp.where(mask, s, NEG)
    m = jnp.max(s, axis=-1, keepdims=True)
    m = jnp.where(m > NEG, m, 0.0)
    p = jnp.where(mask, jnp.exp(s - m), 0.0)
    return p / jnp.maximum(jnp.sum(p, axis=-1, keepdims=True), 1e-30)


def _overlap_matrix():
    c = lax.broadcasted_iota(jnp.int32, (LANES, LANES), 0)
    j = lax.broadcasted_iota(jnp.int32, (LANES, LANES), 1)
    cs = NSA_CMP_STRIDE * c
    bs = NSA_SEL_BLOCK * j
    return ((cs < bs + NSA_SEL_BLOCK) & (cs + NSA_CMP_LEN - 1 >= bs)).astype(f32)


NSA_TK = 256
N_SEL_BLOCKS = 32


def _nsa_prompt_kernel(slope_ref, q_ref, gt_ref, ck_ref, cvt_ref, ks_ref, vst_ref, kw_ref, vwt_ref, o_ref, sel_ref):
    i = pl.program_id(1)
    tq, tk, R, G, ls, dh = NSA_TQ, NSA_TK, NSA_GROUP, NSA_KV_HEADS, NSA_SEL_BLOCK, HEAD_DIM
    W = R * tq
    q_all = q_ref[...]
    gates_t = gt_ref[...].T
    n_lane = lax.broadcasted_iota(jnp.int32, (1, W), 1) & (tq - 1)
    t_lane = i * tq + n_lane
    t_q = i * tq + lax.broadcasted_iota(jnp.int32, (N_SEL_BLOCKS, tq), 1)
    jrow = lax.broadcasted_iota(jnp.int32, (N_SEL_BLOCKS, tq), 0)
    crow = lax.broadcasted_iota(jnp.int32, (N_CMP_ROWS, W), 0)
    visible = (NSA_CMP_STRIDE * crow + NSA_CMP_LEN - 1 <= t_lane) & (crow < N_CMP)
    oj = lax.broadcasted_iota(jnp.int32, (N_SEL_BLOCKS, N_CMP_ROWS), 0) * ls
    oc = lax.broadcasted_iota(jnp.int32, (N_SEL_BLOCKS, N_CMP_ROWS), 1) * NSA_CMP_STRIDE
    overlap_t = ((oc < oj + ls) & (oc + NSA_CMP_LEN - 1 >= oj)).astype(f32)
    krow = lax.broadcasted_iota(jnp.int32, (tk, W), 0)
    rel_i = n_lane - krow
    rel = rel_i.astype(f32)
    diag = (i * tq) // tk
    diag_off = pl.multiple_of(diag * tk, tk)

    def chosen_rows(g, kt):
        parts = [jnp.broadcast_to(sel_ref[g, pl.ds(kt * (tk // ls) + b, 1), :], (ls, W)) for b in range(tk // ls)]
        return jnp.concatenate(parts, axis=0) > 0.5

    groups = []
    for g in range(G):
        gs = slice(g * dh, (g + 1) * dh)
        qs = jnp.concatenate([q_all[:, (g * R + r) * dh:(g * R + r + 1) * dh] for r in range(R)], axis=0)
        slope = slope_ref[g]
        s_c = jnp.where(visible, _dot_nt(ck_ref[0, g], qs, HIGHEST) * SCALE, NEG)
        m_c = jnp.max(s_c, axis=0, keepdims=True)
        p_c = jnp.where(visible, jnp.exp(s_c - jnp.where(m_c > NEG, m_c, 0.0)), 0.0)
        p_c = p_c / jnp.maximum(jnp.sum(p_c, axis=0, keepdims=True), 1e-30)
        o_c = _dot(cvt_ref[0, g].astype(bf16), p_c.astype(bf16))
        p_sum = p_c[:, 0:tq]
        for r in range(1, R):
            p_sum = p_sum + p_c[:, r * tq:(r + 1) * tq]
        imp = _dot(overlap_t, p_sum, HIGHEST)
        sel = _topk_rows(jnp.where(jrow < t_q // ls, imp, NEG), NSA_SEL_TOPK)
        sel_ref[g] = jnp.concatenate([sel] * R, axis=1)
        qb = (qs * SCALE).astype(bf16)
        bias = slope * rel
        d0 = slope * (i * tq - diag * tk).astype(f32)
        dist = rel_i + (i * tq - diag * tk)
        key_blk = (diag * tk + krow) // ls
        s = _dot_nt(ks_ref[pl.ds(diag_off, tk), gs].astype(bf16), qb) - bias - d0
        ok = chosen_rows(g, diag) | ((key_blk == t_lane // ls) & (dist >= 0))
        slc0 = _softmax_first(jnp.where(ok, s, NEG), vst_ref[gs, pl.ds(diag_off, tk)].astype(bf16))
        s = _dot_nt(kw_ref[pl.ds(diag_off, tk), gs].astype(bf16), qb) - bias - d0
        win0 = _softmax_first(jnp.where((dist >= 0) & (dist < NSA_WINDOW), s, NEG),
                              vwt_ref[gs, pl.ds(diag_off, tk)].astype(bf16))
        groups.append((gs, qb, bias, slope, o_c, slc0, win0))

    def slc_body(kt, carry):
        off = pl.multiple_of(kt * tk, tk)
        out = []
        for g, (gs, qb, bias, slope, *_rest) in enumerate(groups):
            s = _dot_nt(ks_ref[pl.ds(off, tk), gs].astype(bf16), qb) - bias - slope * (i * tq - kt * tk).astype(f32)
            s = jnp.where(chosen_rows(g, kt), s, NEG)
            out.append(_softmax_next(s, vst_ref[gs, pl.ds(off, tk)].astype(bf16), *carry[g]))
        return tuple(out)

    def win_body(kt, carry):
        off = pl.multiple_of(kt * tk, tk)
        shift = i * tq - kt * tk
        out = []
        for g, (gs, qb, bias, slope, *_rest) in enumerate(groups):
            s = _dot_nt(kw_ref[pl.ds(off, tk), gs].astype(bf16), qb) - bias - slope * shift.astype(f32)
            s = jnp.where(rel_i + shift < NSA_WINDOW, s, NEG)
            out.append(_softmax_next(s, vwt_ref[gs, pl.ds(off, tk)].astype(bf16), *carry[g]))
        return tuple(out)

    slc = lax.fori_loop(0, diag, slc_body, tuple(grp[5] for grp in groups))
    first_win = jnp.maximum(i * tq - (NSA_WINDOW - 1), 0) // tk
    win = lax.fori_loop(first_win, diag, win_body, tuple(grp[6] for grp in groups))

    for g, (gs, qb, bias, slope, o_c, *_rest) in enumerate(groups):
        def gate_row(branch):
            return jnp.concatenate([gates_t[NSA_BRANCHES * (g * R + r) + branch:NSA_BRANCHES * (g * R + r) + branch + 1]
                                    for r in range(R)], axis=1)
        o_t = (gate_row(0) * o_c + gate_row(1) * (slc[g][2] / slc[g][1]) + gate_row(2) * (win[g][2] / win[g][1]))
        o = o_t.T
        for r in range(R):
            o_ref[:, (g * R + r) * dh:(g * R + r + 1) * dh] = o[r * tq:(r + 1) * tq]


def _alibi_groups(n_groups, group):
    n = n_groups * group
    s = (2.0 ** (-8.0 * np.arange(1, n + 1) / n)).reshape(n_groups, group)
    return jnp.asarray(np.broadcast_to(s[:, :, None], (n_groups, group, LANES)), dtype=f32)


def nsa_prompt(qn, gates, ck, cvt, kvs, vst, kvw, vwt, batch, seq):
    assert seq // NSA_SEL_BLOCK == N_SEL_BLOCKS and (seq - NSA_CMP_LEN) // NSA_CMP_STRIDE + 1 == N_CMP
    nq = seq // NSA_TQ
    G, R = NSA_KV_HEADS, NSA_GROUP
    n = G * R
    slopes = (2.0 ** (-8.0 * np.arange(1, n + 1) / n)).reshape(G, 1, R, 1)
    slope_lanes = jnp.asarray(np.broadcast_to(slopes, (G, 1, R, NSA_TQ)).reshape(G, 1, R * NSA_TQ), dtype=f32)
    full = lambda a: pl.BlockSpec(a.shape, lambda b, i: (0,) * a.ndim)
    return pl.pallas_call(
        _nsa_prompt_kernel,
        grid=(batch, nq),
        in_specs=[full(slope_lanes),
                  pl.BlockSpec((NSA_TQ, NSA_W), lambda b, i: (b * nq + i, 0)),
                  pl.BlockSpec((NSA_TQ, LANES), lambda b, i: (b * nq + i, 0)),
                  pl.BlockSpec((1, G, N_CMP_ROWS, HEAD_DIM), lambda b, i: (b, 0, 0, 0)),
                  pl.BlockSpec((1, G, HEAD_DIM, N_CMP_ROWS), lambda b, i: (b, 0, 0, 0)),
                  pl.BlockSpec((seq, NSA_KVW), lambda b, i: (b, 0)),
                  pl.BlockSpec((NSA_KVW, seq), lambda b, i: (0, b)),
                  pl.BlockSpec((seq, NSA_KVW), lambda b, i: (b, 0)),
                  pl.BlockSpec((NSA_KVW, seq), lambda b, i: (0, b))],
        out_specs=pl.BlockSpec((NSA_TQ, NSA_W), lambda b, i: (b * nq + i, 0)),
        out_shape=jax.ShapeDtypeStruct((batch * seq, NSA_W), f32),
        scratch_shapes=[pltpu.VMEM((G, N_SEL_BLOCKS, R * NSA_TQ), f32)],
        compiler_params=_cparams(("arbitrary", "arbitrary")),
        name="nsa_prompt",
    )(slope_lanes, qn, gates, ck, cvt, kvs, vst, kvw, vwt)


def _head_slopes(n):
    s = 2.0 ** (-8.0 * np.arange(1, n + 1) / n)
    return jnp.asarray(np.broadcast_to(s[:, None], (n, LANES)), dtype=f32)


def _moba_sample_kernel(pt_ref, slope_ref, q_ref, kvn_ref, *refs):
    page_refs, o_ref = refs[:-1], refs[-1]
    n_pages = len(page_refs)
    mb = MOBA_BLOCK
    ppb = mb // PAGE_SIZE
    nb = n_pages // ppb
    t_new = n_pages * PAGE_SIZE
    q = q_ref[0]
    slope = slope_ref[...]
    tok = lax.broadcasted_iota(jnp.int32, (mb, MOBA_HEADS, LANES), 0)
    ones = jnp.ones((HEAD_DIM, LANES), bf16)
    gates, ms, ls, os_ = [], [], [], []
    for j in range(nb):
        k = jnp.concatenate([page_refs[ppb * j + u][pl.ds(0, PAGE_SIZE, stride=2)] for u in range(ppb)], axis=0)
        v = jnp.concatenate([page_refs[ppb * j + u][pl.ds(1, PAGE_SIZE, stride=2)] for u in range(ppb)], axis=0)
        kmean = jnp.sum(k, axis=0) / mb
        gates.append(jnp.sum(q * kmean, axis=-1, keepdims=True))
        dist = (t_new - j * mb - tok).astype(f32)
        kq = (k * q[None]).reshape(mb * MOBA_HEADS, HEAD_DIM).astype(bf16)
        s = _dot(kq, ones).reshape(mb, MOBA_HEADS, LANES) * SCALE - slope[None] * dist
        m = jnp.max(s, axis=0)
        p = jnp.exp(s - m[None])
        ms.append(m)
        ls.append(jnp.sum(p, axis=0))
        os_.append(jnp.sum(p * v, axis=0))
    chosen = []
    for j in range(nb):
        rank = jnp.zeros_like(gates[j])
        for j2 in range(nb):
            if j2 != j:
                ahead = (gates[j2] >= gates[j]) if j2 < j else (gates[j2] > gates[j])
                rank = rank + ahead.astype(f32)
        chosen.append(rank < MOBA_TOPK)
    kn, vn = kvn_ref[0, 0], kvn_ref[0, 1]
    s_own = jnp.sum(q * kn, axis=-1, keepdims=True) * SCALE
    m_all = s_own
    for j in range(nb):
        m_all = jnp.maximum(m_all, jnp.where(chosen[j], ms[j], NEG))
    w_own = jnp.exp(s_own - m_all)
    l_all = w_own
    o_all = w_own * vn
    for j in range(nb):
        w = jnp.where(chosen[j], jnp.exp(ms[j] - m_all), 0.0)
        l_all = l_all + w * ls[j]
        o_all = o_all + w * os_[j]
    o_ref[0] = o_all / l_all


def moba_sample(q3, kvn4, cache3, page_table):
    nseq, n_pages = page_table.shape
    rows = 2 * PAGE_SIZE
    page_spec = lambda p: pl.BlockSpec((rows, MOBA_HEADS, HEAD_DIM), lambda b, pt: (pt[b, p], 0, 0))
    return pl.pallas_call(
        _moba_sample_kernel,
        grid_spec=pltpu.PrefetchScalarGridSpec(
            num_scalar_prefetch=1, grid=(nseq,),
            in_specs=[pl.BlockSpec((MOBA_HEADS, LANES), lambda b, pt: (0, 0)),
                      pl.BlockSpec((1, MOBA_HEADS, HEAD_DIM), lambda b, pt: (b, 0, 0)),
                      pl.BlockSpec((1, 2, MOBA_HEADS, HEAD_DIM), lambda b, pt: (b, 0, 0, 0))]
                     + [page_spec(p) for p in range(n_pages)],
            out_specs=pl.BlockSpec((1, MOBA_HEADS, HEAD_DIM), lambda b, pt: (b, 0, 0))),
        out_shape=jax.ShapeDtypeStruct((nseq, MOBA_HEADS, HEAD_DIM), f32),
        compiler_params=_cparams(("arbitrary",)),
        name="moba_sample",
    )(page_table, _head_slopes(MOBA_HEADS), q3, kvn4, *([cache3] * n_pages))


KV_ROWS = 2 * NSA_KV_HEADS


def _nsa_sample_compress_kernel(pt_ref, pe_ref, w1k_ref, w1v_ref, w2k_ref, w2v_ref, *refs):
    page_refs, ckv_ref = refs[:-1], refs[-1]
    per_page = PAGE_SIZE // CMP_HALF
    G = NSA_KV_HEADS
    for kv, (w1_ref, w2_ref) in enumerate(((w1k_ref, w2k_ref), (w1v_ref, w2v_ref))):
        pe = pe_ref[kv]
        xa, xb = [], []
        for g in range(G):
            pa, pb = [], []
            for l in range(CMP_HALF):
                xl = jnp.concatenate(
                    [pr[pl.ds(KV_ROWS * l + G * kv + g, per_page, stride=KV_ROWS * CMP_HALF), :] for pr in page_refs], axis=0)
                pa.append((xl + pe[l:l + 1]).astype(bf16))
                pb.append((xl + pe[CMP_HALF + l:CMP_HALF + l + 1]).astype(bf16))
            xa.append(jnp.concatenate(pa, axis=1))
            xb.append(jnp.concatenate(pb, axis=1))
        comp = _compress_rows(jnp.concatenate(xa, axis=0), jnp.concatenate(xb, axis=0), w1_ref, w2_ref)
        for g in range(G):
            ckv_ref[0, kv, g] = comp[g * N_CMP_ROWS:(g + 1) * N_CMP_ROWS]


SELECT_BATCH = 8


def _nsa_sample_select_kernel(q_ref, ckv_ref, oc_ref, sel_ref, *, t_new):
    sb = q_ref.shape[0]
    G, R, H = NSA_KV_HEADS, NSA_GROUP, NSA_HEADS
    row = lax.broadcasted_iota(jnp.int32, (H, LANES), 0)
    lane = lax.broadcasted_iota(jnp.int32, (sb * H, LANES), 1)
    n_cmp = (t_new + 1 - NSA_CMP_LEN) // NSA_CMP_STRIDE + 1
    visible = (lane < n_cmp) & (NSA_CMP_STRIDE * lane + NSA_CMP_LEN - 1 <= t_new)
    first = row < R
    s_c = jnp.concatenate(
        [jnp.where(first, _dot_nt(q_ref[b], ckv_ref[b, 0, 0], HIGHEST), _dot_nt(q_ref[b], ckv_ref[b, 0, 1], HIGHEST))
         for b in range(sb)], axis=0) * SCALE
    p_c = _masked_softmax(s_c, visible)
    p_b = p_c.astype(bf16)
    for b in range(sb):
        pb = p_b[b * H:(b + 1) * H]
        oc_ref[b] = jnp.where(first, _dot(pb, ckv_ref[b, 1, 0].astype(bf16)), _dot(pb, ckv_ref[b, 1, 1].astype(bf16)))
    gi = lax.broadcasted_iota(jnp.int32, (sb * G, sb * H), 0)
    gj = lax.broadcasted_iota(jnp.int32, (sb * G, sb * H), 1)
    p_sum = _dot((gj // R == gi).astype(f32), p_c, HIGHEST)
    imp = _dot(p_sum, _overlap_matrix(), HIGHEST)
    lane_g = lax.broadcasted_iota(jnp.int32, (sb * G, LANES), 1)
    cur = jnp.where(lane_g < t_new // NSA_SEL_BLOCK, imp, NEG)
    out = jnp.zeros((sb * G, LANES), jnp.int32)
    for s in range(NSA_SEL_TOPK):
        m = jnp.max(cur, axis=-1, keepdims=True)
        idx = jnp.min(jnp.where((cur == m) & (m > NEG), lane_g, LANES), axis=-1, keepdims=True)
        found = idx < LANES
        out = jnp.where(lane_g == s, jnp.where(found, idx, 0), out)
        out = jnp.where(lane_g == NSA_SEL_TOPK + s, found.astype(jnp.int32), out)
        cur = jnp.where(lane_g == idx, NEG, cur)
    sel_ref[...] = out


def nsa_sample_cmp(q3, cache2, page_table, pe2, w1k, w1v, w2k, w2v):
    nseq, n_pages = page_table.shape
    G = NSA_KV_HEADS
    rows = PAGE_SIZE * KV_ROWS
    page_spec = lambda p: pl.BlockSpec((rows, HEAD_DIM), lambda b, pt: (pt[b, p], 0))
    full = lambda a: pl.BlockSpec(a.shape, lambda b, pt: (0,) * a.ndim)
    ckv = pl.pallas_call(
        _nsa_sample_compress_kernel,
        grid_spec=pltpu.PrefetchScalarGridSpec(
            num_scalar_prefetch=1, grid=(nseq,),
            in_specs=[full(pe2), full(w1k), full(w1v), full(w2k), full(w2v)]
                     + [page_spec(p) for p in range(n_pages)],
            out_specs=pl.BlockSpec((1, 2, G, N_CMP_ROWS, HEAD_DIM), lambda b, pt: (b, 0, 0, 0, 0))),
        out_shape=jax.ShapeDtypeStruct((nseq, 2, G, N_CMP_ROWS, HEAD_DIM), f32),
        compiler_params=_cparams(("arbitrary",)),
        name="nsa_sample_compress",
    )(page_table, pe2, w1k, w1v, w2k, w2v, *([cache2] * n_pages))
    sb = SELECT_BATCH
    return pl.pallas_call(
        functools.partial(_nsa_sample_select_kernel, t_new=n_pages * PAGE_SIZE),
        grid=(nseq // sb,),
        in_specs=[pl.BlockSpec((sb, NSA_HEADS, HEAD_DIM), lambda i: (i, 0, 0)),
                  pl.BlockSpec((sb, 2, G, N_CMP_ROWS, HEAD_DIM), lambda i: (i, 0, 0, 0, 0))],
        out_specs=[pl.BlockSpec((sb, NSA_HEADS, HEAD_DIM), lambda i: (i, 0, 0)),
                   pl.BlockSpec((sb * G, LANES), lambda i: (i, 0))],
        out_shape=[jax.ShapeDtypeStruct((nseq, NSA_HEADS, HEAD_DIM), f32),
                   jax.ShapeDtypeStruct((nseq * G, LANES), jnp.int32)],
        compiler_params=_cparams(("arbitrary",)),
        name="nsa_sample_select",
    )(q3, ckv)


def _decode_attend(q, slope, keys, vals, pos, valid, k_own, v_own, t_new):
    s = _dot_nt(q.astype(bf16), keys.astype(bf16)) * SCALE - slope * (t_new - pos).astype(f32)
    s = jnp.where(valid, s, NEG)
    s_own = jnp.sum(q * k_own, axis=-1, keepdims=True) * SCALE
    m = jnp.maximum(jnp.max(s, axis=-1, keepdims=True), s_own)
    p = jnp.exp(s - m)
    p_own = jnp.exp(s_own - m)
    denom = jnp.sum(p, axis=-1, keepdims=True) + p_own
    return (_dot(p.astype(bf16), vals.astype(bf16)) + p_own * v_own) / denom


def _nsa_sample_attn_kernel(pt_ref, sel_ref, slope_ref, q_ref, gt_ref, oc_ref, ksn_ref, kwn_ref, win_ref, *refs):
    blk_refs, (o_ref, wout_ref) = refs[:-2], refs[-2:]
    b = pl.program_id(0)
    G, R, ls, K = NSA_KV_HEADS, NSA_GROUP, NSA_SEL_BLOCK, NSA_SEL_TOPK
    t_new = pt_ref.shape[1] * PAGE_SIZE
    q = q_ref[0]
    slope = slope_ref[:, :1]
    row = lax.broadcasted_iota(jnp.int32, (G * R, HEAD_DIM), 0)
    own_rows = lambda ref, kv: jnp.where(row < R, ref[0, G * kv:G * kv + 1], ref[0, G * kv + 1:G * kv + 2])
    lane_s = lax.broadcasted_iota(jnp.int32, (1, K * ls), 1)
    n_win = win_ref.shape[0] // KV_ROWS
    lane_w = lax.broadcasted_iota(jnp.int32, (1, n_win), 1)
    pos_w = t_new - n_win + lane_w
    o_s = jnp.zeros((G * R, HEAD_DIM), f32)
    o_w = jnp.zeros((G * R, HEAD_DIM), f32)
    for g in range(G):
        keys = jnp.concatenate([blk_refs[g * K + s][pl.ds(g, ls, stride=KV_ROWS), :] for s in range(K)], axis=0)
        vals = jnp.concatenate([blk_refs[g * K + s][pl.ds(G + g, ls, stride=KV_ROWS), :] for s in range(K)], axis=0)
        pos = jnp.zeros((1, K * ls), jnp.int32)
        valid = jnp.zeros((1, K * ls), jnp.int32)
        for s in range(K):
            here = lane_s // ls == s
            pos = jnp.where(here, sel_ref[b, g * 2 * K + s] * ls + lane_s - s * ls, pos)
            valid = jnp.where(here, sel_ref[b, g * 2 * K + K + s], valid)
        mine = (row >= g * R) & (row < (g + 1) * R)
        o_s = jnp.where(mine, _decode_attend(q, slope, keys, vals, pos, valid > 0, own_rows(ksn_ref, 0), own_rows(ksn_ref, 1), t_new), o_s)
        keys_w = win_ref[pl.ds(g, n_win, stride=KV_ROWS), :]
        vals_w = win_ref[pl.ds(G + g, n_win, stride=KV_ROWS), :]
        ok_w = (pos_w > t_new - NSA_WINDOW) & (pos_w >= 0)
        o_w = jnp.where(mine, _decode_attend(q, slope, keys_w, vals_w, pos_w, ok_w, own_rows(kwn_ref, 0), own_rows(kwn_ref, 1), t_new), o_w)
    gt = gt_ref[0]
    o_ref[0] = gt[:, 0:1] * oc_ref[0] + gt[:, 1:2] * o_s + gt[:, 2:3] * o_w
    total = win_ref.shape[0]
    shifted = pltpu.roll(win_ref[...], total - KV_ROWS, 0)
    new8 = jnp.concatenate([kwn_ref[0], kwn_ref[0]], axis=0)
    row8 = lax.broadcasted_iota(jnp.int32, (2 * KV_ROWS, HEAD_DIM), 0)
    wout_ref[pl.ds(0, total - 2 * KV_ROWS), :] = shifted[:total - 2 * KV_ROWS]
    wout_ref[pl.ds(total - 2 * KV_ROWS, 2 * KV_ROWS), :] = jnp.where(row8 >= KV_ROWS, new8, shifted[total - 2 * KV_ROWS:])


def nsa_sample_attn(q3, gates3, o_c, ksn, kwn, slc2, win2, page_table, sel_flat):
    nseq = page_table.shape[0]
    G, K, ls = NSA_KV_HEADS, NSA_SEL_TOPK, NSA_SEL_BLOCK
    blocks_per_page = PAGE_SIZE // ls
    n_blocks = page_table.shape[1] * blocks_per_page
    win_rows = win2.shape[0] // nseq

    def blk_spec(g, s):
        def index(b, pt, sel):
            bb = jnp.minimum(b, nseq - 1)
            blk = jnp.clip(sel[bb, g * 2 * K + s], 0, n_blocks - 1)
            return (pt[bb, blk // blocks_per_page] * blocks_per_page + blk % blocks_per_page, 0)
        return pl.BlockSpec((ls * KV_ROWS, HEAD_DIM), index)

    per_seq = lambda shape: pl.BlockSpec((1,) + shape, lambda b, pt, sel: (b,) + (0,) * len(shape))
    return pl.pallas_call(
        _nsa_sample_attn_kernel,
        grid_spec=pltpu.PrefetchScalarGridSpec(
            num_scalar_prefetch=2, grid=(nseq,),
            in_specs=[pl.BlockSpec((NSA_HEADS, LANES), lambda b, pt, sel: (0, 0)),
                      per_seq((NSA_HEADS, HEAD_DIM)), per_seq((NSA_HEADS, LANES)), per_seq((NSA_HEADS, HEAD_DIM)),
                      per_seq((KV_ROWS, HEAD_DIM)), per_seq((KV_ROWS, HEAD_DIM)),
                      pl.BlockSpec((win_rows, HEAD_DIM), lambda b, pt, sel: (b, 0))]
                     + [blk_spec(g, s) for g in range(G) for s in range(K)],
            out_specs=[per_seq((NSA_HEADS, HEAD_DIM)),
                       pl.BlockSpec((win_rows, HEAD_DIM), lambda b, pt, sel: (b, 0))]),
        out_shape=[jax.ShapeDtypeStruct((nseq, NSA_HEADS, HEAD_DIM), f32),
                   jax.ShapeDtypeStruct(win2.shape, f32)],
        compiler_params=_cparams(("arbitrary",)),
        name="nsa_sample_attn",
    )(page_table, sel_flat, _head_slopes(NSA_HEADS), q3, gates3, o_c, ksn, kwn, win2, *([slc2] * (G * K)))


def _mid_kernel(x_ref, om_ref, on_ref, w_ref, g_ref, gate_ref, sh_ref, sc_ref, x1_ref, h2_ref):
    proj = (_dot(om_ref[...].astype(bf16), w_ref[:MOBA_W, :]) + _dot(on_ref[...].astype(bf16), w_ref[MOBA_W:, :]))
    x1 = x_ref[...] + gate_ref[0] * proj
    x1_ref[...] = x1
    h2_ref[...] = _rms_mod(x1, g_ref[...], sh_ref[0], sc_ref[0]).astype(bf16)


def mid_block(x, o_m, o_n, w_out_b, g_norm2, mods3, tm, rows_per_mod):
    t = x.shape[0]
    r = mods3.shape[1]
    tiles_per_mod = rows_per_mod // tm
    mod_spec = lambda which: pl.BlockSpec((1, r, D_MODEL), lambda i: (i // tiles_per_mod, 0, which))
    return pl.pallas_call(
        _mid_kernel,
        grid=(t // tm,),
        in_specs=[pl.BlockSpec((tm, D_MODEL), lambda i: (i, 0)),
                  pl.BlockSpec((tm, MOBA_W), lambda i: (i, 0)),
                  pl.BlockSpec((tm, NSA_W), lambda i: (i, 0)),
                  pl.BlockSpec((MOBA_W + NSA_W, D_MODEL), lambda i: (0, 0)),
                  pl.BlockSpec((1, D_MODEL), lambda i: (0, 0)),
                  mod_spec(2), mod_spec(3), mod_spec(4)],
        out_specs=[pl.BlockSpec((tm, D_MODEL), lambda i: (i, 0)),
                   pl.BlockSpec((tm, D_MODEL), lambda i: (i, 0))],
        out_shape=[jax.ShapeDtypeStruct((t, D_MODEL), f32), jax.ShapeDtypeStruct((t, D_MODEL), bf16)],
        compiler_params=_cparams(("arbitrary",)),
        name="mid_block",
    )(x, o_m, o_n, w_out_b, g_norm2.reshape(1, D_MODEL), mods3, mods3, mods3)


PEER_HALF = PEER_QDIM // 2
PEER_A_FULL = 8


def _top_values(s, k):
    tops = []
    cur = s
    for _ in range(k):
        m = jnp.max(cur, axis=0, keepdims=True)
        tops.append(m)
        cur = jnp.where(cur == m, NEG, cur)
    return jnp.concatenate(tops, axis=0)


def _peer_route_kernel(h_ref, wq_ref, keys_ref, s1_ref, thr_ref, e1_ref, coef_ref):
    q = _dot(h_ref[...], wq_ref[...])
    k = PEER_TOPK
    for h in range(PEER_HEADS):
        base = h * PEER_QDIM
        s0 = _dot_nt(keys_ref[h, 0], q[:, base:base + PEER_HALF], HIGHEST)
        s1 = _dot_nt(keys_ref[h, 1], q[:, base + PEER_HALF:base + PEER_QDIM], HIGHEST)
        top0 = _top_values(s0, k)
        top1 = _top_values(s1, k)
        cand = jnp.concatenate([top0[a:a + 1] + top1 for a in range(PEER_A_FULL)]
                               + [top0[PEER_A_FULL:] + top1[0:1]], axis=0)
        best = _top_values(cand, k)
        tau = best[k - 1:k]
        z = jnp.sum(jnp.exp(best - best[0:1]), axis=0, keepdims=True)
        thr = jnp.full(s0.shape, jnp.inf, f32)
        for a in range(k):
            thr_a = jnp.min(jnp.where(top0[a:a + 1] + top1 >= tau, top1, jnp.inf), axis=0, keepdims=True)
            thr = jnp.where(s0 == top0[a:a + 1], thr_a, thr)
        outs = ((s1_ref, s1), (thr_ref, thr), (e1_ref, jnp.exp(s1 - top1[0:1])), (coef_ref, jnp.exp(s0 - top0[0:1]) / z))
        for ref, val in outs:
            for c in range(val.shape[1] // LANES):
                ref[h, c] = val[:, c * LANES:(c + 1) * LANES]


def peer_route(h2, wq_b, keys, tm):
    t = h2.shape[0]
    out_spec = pl.BlockSpec((PEER_HEADS, tm // LANES, PEER_KEYS, LANES), lambda i: (0, i, 0, 0))
    out_shape = jax.ShapeDtypeStruct((PEER_HEADS, t // LANES, PEER_KEYS, LANES), f32)
    return pl.pallas_call(
        _peer_route_kernel,
        grid=(t // tm,),
        in_specs=[pl.BlockSpec((tm, D_MODEL), lambda i: (i, 0)),
                  pl.BlockSpec(wq_b.shape, lambda i: (0, 0)),
                  pl.BlockSpec(keys.shape, lambda i: (0, 0, 0, 0))],
        out_specs=[out_spec] * 4,
        out_shape=[out_shape] * 4,
        compiler_params=_cparams(("arbitrary",)),
        name="peer_route",
    )(h2, wq_b, keys)


PEER_EXPERT_TILE = 1024
PEER_SLAB = 512


def _peer_expert_kernel(ht_ref, u_ref, vt_ref, s1_ref, thr_ref, e1_ref, coef_ref, x1_ref, gate_ref, gf_ref,
                        y_ref, acc_ref, p_ref):
    e = pl.program_id(1)
    te = u_ref.shape[0]
    tm = ht_ref.shape[1]
    rows_per_step = te // PEER_KEYS

    @pl.when(e == 0)
    def _():
        acc_ref[...] = jnp.zeros_like(acc_ref)

    for slab in range(te // PEER_SLAB):
        act = _gelu(_dot(u_ref[slab * PEER_SLAB:(slab + 1) * PEER_SLAB, :], ht_ref[...]))
        for rr in range(PEER_SLAB // PEER_KEYS):
            r = slab * (PEER_SLAB // PEER_KEYS) + rr
            i0 = e * rows_per_step + r
            for c in range(tm // LANES):
                w = jnp.zeros((PEER_KEYS, LANES), f32)
                for h in range(PEER_HEADS):
                    thr = thr_ref[h, c, pl.ds(i0, 1), :]
                    coef = coef_ref[h, c, pl.ds(i0, 1), :]
                    w = w + jnp.where(s1_ref[h, c] >= thr, e1_ref[h, c] * coef, 0.0)
                cs = slice(c * LANES, (c + 1) * LANES)
                p_ref[r * PEER_KEYS:(r + 1) * PEER_KEYS, cs] = (
                    w * act[rr * PEER_KEYS:(rr + 1) * PEER_KEYS, cs]).astype(bf16)
    acc_ref[...] += _dot(vt_ref[...], p_ref[...])

    @pl.when(e == pl.num_programs(1) - 1)
    def _():
        y = x1_ref[...] + gate_ref[0] * acc_ref[...].T
        y_ref[...] = (y * lax.rsqrt(jnp.mean(y * y, axis=-1, keepdims=True) + RMS_EPS)) * gf_ref[...]


def peer_experts(h2t, u_b, vt_b, route, x1, mods3, g_final, tm, te, rows_per_mod):
    t = h2t.shape[1]
    r = mods3.shape[1]
    n_tiles = u_b.shape[0] // te
    tiles_per_mod = rows_per_mod // tm
    once = dict(pipeline_mode=pl.Buffered(1))
    route_spec = pl.BlockSpec((PEER_HEADS, tm // LANES, PEER_KEYS, LANES), lambda i, e: (0, i, 0, 0), **once)
    return pl.pallas_call(
        _peer_expert_kernel,
        grid=(t // tm, n_tiles),
        in_specs=[pl.BlockSpec((D_MODEL, tm), lambda i, e: (0, i), **once),
                  pl.BlockSpec((te, D_MODEL), lambda i, e: (e, 0)),
                  pl.BlockSpec((D_MODEL, te), lambda i, e: (0, e)),
                  route_spec, route_spec, route_spec, route_spec,
                  pl.BlockSpec((tm, D_MODEL), lambda i, e: (i, 0), **once),
                  pl.BlockSpec((1, r, D_MODEL), lambda i, e: (i // tiles_per_mod, 0, 5)),
                  pl.BlockSpec((1, D_MODEL), lambda i, e: (0, 0))],
        out_specs=pl.BlockSpec((tm, D_MODEL), lambda i, e: (i, 0)),
        out_shape=jax.ShapeDtypeStruct((t, D_MODEL), f32),
        scratch_shapes=[pltpu.VMEM((D_MODEL, tm), f32), pltpu.VMEM((te, tm), bf16)],
        compiler_params=_cparams(("arbitrary", "arbitrary")),
        name="peer_experts",
    )(h2t, u_b, vt_b, *route, x1, mods3, g_final.reshape(1, D_MODEL))


def _group_forward(x2, mods3, tm, rows_per_mod, w, attend):
    proj = in_projection(x2, mods3, w["g_norm1"], w["w_in"], tm, rows_per_mod)
    o_m, o_n = attend(proj)
    x1, h2 = mid_block(x2, o_m, o_n, w["w_out"], w["g_norm2"], mods3, min(tm, 256), rows_per_mod)
    route = peer_route(h2, w["peer_w_q"], w["peer_keys"], min(tm, 256))
    y = peer_experts(h2.T, w["peer_u"], w["peer_vt"], route, x1, mods3, w["g_final"], tm, PEER_EXPERT_TILE, rows_per_mod)
    return proj, y


def kernel(x_prompt, x_sample, cache_moba_kv, cache_nsa_cmp_kv, cache_nsa_slc_kv, state_nsa_win_kv, page_table,
           c_prompt, c_sample, w_ada, b_ada, g_norm1, w_in, cmp_pe_k, cmp_w1_k, cmp_w2_k, cmp_pe_v, cmp_w1_v,
           cmp_w2_v, w_out, g_norm2, peer_w_q, peer_keys, peer_u, peer_v, g_final):
    assert w_ada.shape[0] == 1, "single layer"
    batch, seq, _ = x_prompt.shape
    nseq, dec_seq, _ = x_sample.shape
    assert dec_seq == 1 and state_nsa_win_kv.shape[2] == NSA_WINDOW and seq >= NSA_WINDOW
    G, H, dh = NSA_KV_HEADS, MOBA_HEADS, HEAD_DIM

    c_all = jnp.concatenate([c_prompt, c_sample], axis=0)
    pad = (-c_all.shape[0]) % 8
    mods = ada_mods(jnp.pad(c_all, ((0, pad), (0, 0))), w_ada[0], b_ada[0])
    mods_p = mods[:batch].reshape(batch, 1, N_MOD * D_MODEL)
    mods_s = mods[batch:batch + nseq].reshape(1, nseq, N_MOD * D_MODEL)

    p_in = w_in.shape[2]
    pe2 = jnp.stack([cmp_pe_k[0], cmp_pe_v[0]])
    w12 = jnp.stack([cmp_w1_k[0], cmp_w1_v[0]]).astype(bf16)
    w22 = jnp.stack([cmp_w2_k[0], cmp_w2_v[0]]).astype(bf16)
    w = dict(
        g_norm1=g_norm1[0], g_norm2=g_norm2[0], g_final=g_final,
        w_in=jnp.pad(w_in[0], ((0, 0), (0, IN_COLS - p_in))).astype(bf16),
        w_out=w_out[0].astype(bf16),
        peer_w_q=peer_w_q[0].astype(bf16), peer_keys=peer_keys[0],
        peer_u=peer_u[0].astype(bf16), peer_vt=peer_v[0].T.astype(bf16),
    )

    def attend_prompt(proj):
        qm, kvm, qn, kvc, kvs, kvw, gates = proj[:7]
        ckv = compress_prompt(kvc, pe2, w12, w22, batch, seq)
        o_m = moba_prompt(qm, kvm, kvm[:, MOBA_W:].T, batch, seq)
        o_n = nsa_prompt(qn, gates, ckv[:, 0], jnp.swapaxes(ckv[:, 1], 2, 3),
                         kvs, kvs[:, NSA_KVW:].T, kvw, kvw[:, NSA_KVW:].T, batch, seq)
        return o_m, o_n

    win_out = []

    def attend_sample(proj):
        qm, kvm, qn, kvc, kvs, kvw, gates = proj[:7]
        o_m = moba_sample(qm.reshape(nseq, H, dh), kvm.reshape(nseq, 2, H, dh), cache_moba_kv.reshape(-1, H, dh), page_table)
        qn3 = qn.reshape(nseq, NSA_HEADS, dh)
        o_c, sel = nsa_sample_cmp(qn3, cache_nsa_cmp_kv.reshape(-1, dh), page_table, pe2,
                                  w12[0:1], w12[1:2], w22[0:1], w22[1:2])
        sel_flat = sel[:, :2 * NSA_SEL_TOPK].reshape(nseq, G * 2 * NSA_SEL_TOPK)
        gates3 = jnp.pad(gates[:, :NSA_HEADS * NSA_BRANCHES].reshape(nseq, NSA_HEADS, NSA_BRANCHES),
                         ((0, 0), (0, 0), (0, LANES - NSA_BRANCHES)))
        o_n, wout = nsa_sample_attn(qn3, gates3, o_c, kvs.reshape(nseq, KV_ROWS, dh), kvw.reshape(nseq, KV_ROWS, dh),
                                    cache_nsa_slc_kv.reshape(-1, dh), state_nsa_win_kv.reshape(-1, dh), page_table, sel_flat)
        win_out.append(wout)
        return o_m.reshape(nseq, MOBA_W), o_n.reshape(nseq, NSA_W)

    proj_p, y_p = _group_forward(x_prompt.reshape(batch * seq, D_MODEL), mods_p, 512, seq, w, attend_prompt)
    proj_s, y_s = _group_forward(x_sample.reshape(nseq, D_MODEL), mods_s, nseq, nseq, w, attend_sample)

    kv_p = lambda a, nh: a.reshape(1, batch, seq, 2, nh, dh)
    kv_s = lambda a, nh: a.reshape(1, nseq, 1, 2, nh, dh)
    return (y_p.reshape(batch, seq, D_MODEL), y_s.reshape(nseq, 1, D_MODEL),
            kv_p(proj_p[7], H), kv_s(proj_s[7], H),
            kv_p(proj_p[8], G), kv_s(proj_s[8], G),
            kv_p(proj_p[9], G), kv_s(proj_s[9], G),
            kv_p(proj_p[10], G)[:, :, seq - NSA_WINDOW:],
            win_out[0].reshape(state_nsa_win_kv.shape))
```

```python
import functools

import jax
import jax.numpy as jnp
import numpy as np
from jax import lax
from jax.experimental import pallas as pl
from jax.experimental.pallas import tpu as pltpu

f32 = jnp.float32
bf16 = jnp.bfloat16
HIGHEST = lax.Precision.HIGHEST

D_MODEL = 2048
HEAD_DIM = 128
MOBA_HEADS = 8
NSA_HEADS = 8
NSA_KV_HEADS = 2
NSA_GROUP = 4
MOBA_W = MOBA_HEADS * HEAD_DIM
NSA_W = NSA_HEADS * HEAD_DIM
NSA_KVW = NSA_KV_HEADS * HEAD_DIM
MOBA_BLOCK = 256
MOBA_TOPK = 3
NSA_CMP_LEN = 32
NSA_CMP_STRIDE = 16
NSA_CMP_HIDDEN = 256
NSA_SEL_BLOCK = 64
NSA_SEL_TOPK = 4
NSA_WINDOW = 512
NSA_BRANCHES = 3
PEER_KEYS = 128
PEER_HEADS = 8
PEER_QDIM = 256
PEER_TOPK = 16
N_MOD = 6
RMS_EPS = 1e-6
PAGE_SIZE = 128
SCALE = HEAD_DIM ** -0.5
NEG = -jnp.inf
LANES = 128

IN_TILE = 512
IN_WIDE_TILES = 11
IN_COLS = IN_WIDE_TILES * IN_TILE + LANES
VMEM_LIMIT = 56 * 1024 * 1024


def _cparams(sem):
    return pltpu.CompilerParams(dimension_semantics=sem, vmem_limit_bytes=VMEM_LIMIT)


def _gelu(x):
    z2 = np.float32(2.0 * np.sqrt(2.0 / np.pi)) * (x + 0.044715 * (x * x * x))
    return x / (1.0 + jnp.exp(-z2))


def _dot_nt(a, b, precision=None):
    return lax.dot_general(a, b, (((1,), (1,)), ((), ())), precision=precision, preferred_element_type=f32)


def _dot(a, b, precision=None):
    return jnp.dot(a, b, precision=precision, preferred_element_type=f32)


def _ada_kernel(c_ref, w_ref, b_ref, o_ref):
    c = c_ref[...]
    a = c * jax.nn.sigmoid(c)
    o_ref[...] = _dot(a, w_ref[...], HIGHEST) + b_ref[...]


def ada_mods(c, w_ada, b_ada):
    rows = c.shape[0]
    n = w_ada.shape[1]
    tn = 1024
    return pl.pallas_call(
        _ada_kernel,
        grid=(n // tn,),
        in_specs=[pl.BlockSpec((rows, D_MODEL), lambda j: (0, 0)),
                  pl.BlockSpec((D_MODEL, tn), lambda j: (0, j)),
                  pl.BlockSpec((1, tn), lambda j: (0, j))],
        out_specs=pl.BlockSpec((rows, tn), lambda j: (0, j)),
        out_shape=jax.ShapeDtypeStruct((rows, n), f32),
        compiler_params=_cparams(("arbitrary",)),
        name="ada_mods",
    )(c, w_ada, b_ada.reshape(1, n))


def _rms_mod(x, g, shift, scale):
    y = x * lax.rsqrt(jnp.mean(x * x, axis=-1, keepdims=True) + RMS_EPS)
    return (y * g) * (1.0 + scale) + shift


def _inproj_kernel(x_ref, g_ref, sh_ref, sc_ref, w_ref, wg_ref,
                   qm_ref, kvm_ref, qn_ref, kvc_ref, kvs_ref, kvw_ref, gt_ref,
                   kvm_out_ref, kvc_out_ref, kvs_out_ref, kvw_out_ref, h_scr):
    j = pl.program_id(1)
    tm = x_ref.shape[0]
    heads_per_tile = IN_TILE // HEAD_DIM

    @pl.when(j == 0)
    def _():
        h = _rms_mod(x_ref[...], g_ref[...], sh_ref[0], sc_ref[0]).astype(bf16)
        h_scr[...] = h
        gt_ref[...] = jax.nn.sigmoid(_dot(h, wg_ref[...]))

    acc = _dot(h_scr[...], w_ref[...])

    @pl.when(j < 2)
    def _():
        qm_ref[...] = acc

    for p in range(2 * MOBA_W // IN_TILE):
        @pl.when(j == 2 + p)
        def _(p=p):
            kvm_ref[...] = acc
            kvm_out_ref[:, heads_per_tile * p:heads_per_tile * (p + 1), :] = acc.reshape(tm, heads_per_tile, HEAD_DIM)

    @pl.when((j >= 6) & (j < 8))
    def _():
        qn_ref[...] = acc

    for p, (flat_ref, out_ref) in enumerate(((kvc_ref, kvc_out_ref), (kvs_ref, kvs_out_ref), (kvw_ref, kvw_out_ref))):
        @pl.when(j == 8 + p)
        def _(flat_ref=flat_ref, out_ref=out_ref):
            flat_ref[...] = acc
            out_ref[...] = acc.reshape(tm, 2, NSA_KV_HEADS, HEAD_DIM)


def in_projection(x, mods3, g_norm1, w_in_p, tm, rows_per_mod):
    t = x.shape[0]
    r = mods3.shape[1]
    tiles_per_mod = rows_per_mod // tm
    mod_spec = lambda which: pl.BlockSpec((1, r, D_MODEL), lambda i, j: (i // tiles_per_mod, 0, which))
    clip = lambda j, lo, n: jnp.clip(j - lo, 0, n - 1)
    out_shapes = [jax.ShapeDtypeStruct((t, w), f32) for w in (MOBA_W, 2 * MOBA_W, NSA_W, 2 * NSA_KVW, 2 * NSA_KVW, 2 * NSA_KVW, LANES)]
    out_specs = [
        pl.BlockSpec((tm, IN_TILE), lambda i, j: (i, clip(j, 0, 2))),
        pl.BlockSpec((tm, IN_TILE), lambda i, j: (i, clip(j, 2, 4))),
        pl.BlockSpec((tm, IN_TILE), lambda i, j: (i, clip(j, 6, 2))),
        pl.BlockSpec((tm, IN_TILE), lambda i, j: (i, 0)),
        pl.BlockSpec((tm, IN_TILE), lambda i, j: (i, 0)),
        pl.BlockSpec((tm, IN_TILE), lambda i, j: (i, 0)),
        pl.BlockSpec((tm, LANES), lambda i, j: (i, 0)),
        pl.BlockSpec((tm, 2 * MOBA_HEADS, HEAD_DIM), lambda i, j: (i, 0, 0)),
    ] + [pl.BlockSpec((tm, 2, NSA_KV_HEADS, HEAD_DIM), lambda i, j: (i, 0, 0, 0))] * 3
    out_shapes += [jax.ShapeDtypeStruct((t, 2 * MOBA_HEADS, HEAD_DIM), f32)]
    out_shapes += [jax.ShapeDtypeStruct((t, 2, NSA_KV_HEADS, HEAD_DIM), f32)] * 3
    return pl.pallas_call(
        _inproj_kernel,
        grid=(t // tm, IN_WIDE_TILES),
        in_specs=[pl.BlockSpec((tm, D_MODEL), lambda i, j: (i, 0)),
                  pl.BlockSpec((1, D_MODEL), lambda i, j: (0, 0)),
                  mod_spec(0), mod_spec(1),
                  pl.BlockSpec((D_MODEL, IN_TILE), lambda i, j: (0, j)),
                  pl.BlockSpec((D_MODEL, LANES), lambda i, j: (0, IN_WIDE_TILES * IN_TILE // LANES))],
        out_specs=out_specs,
        out_shape=out_shapes,
        scratch_shapes=[pltpu.VMEM((tm, D_MODEL), bf16)],
        compiler_params=_cparams(("arbitrary", "arbitrary")),
        name="in_projection",
    )(x, g_norm1.reshape(1, D_MODEL), mods3, mods3, w_in_p, w_in_p)


def _topk_rows(score, k):
    rows = score.shape[0]
    sub = lax.broadcasted_iota(jnp.int32, score.shape, 0)
    sel = jnp.zeros(score.shape, f32)
    g = score
    for _ in range(k):
        m = jnp.max(g, axis=0, keepdims=True)
        hit = (g == m) & (m > NEG)
        idx = jnp.min(jnp.where(hit, sub, rows), axis=0, keepdims=True)
        pick = sub == idx
        sel = jnp.where(pick, 1.0, sel)
        g = jnp.where(pick, NEG, g)
    return sel


def _softmax_first(s, vt):
    m = jnp.max(s, axis=0, keepdims=True)
    p = jnp.exp(s - m)
    return m, jnp.sum(p, axis=0, keepdims=True), _dot(vt, p.astype(bf16))


def _softmax_next(s, vt, m_i, l_i, acc):
    m_new = jnp.maximum(m_i, jnp.max(s, axis=0, keepdims=True))
    alpha = jnp.exp(m_i - m_new)
    p = jnp.exp(s - m_new)
    return m_new, alpha * l_i + jnp.sum(p, axis=0, keepdims=True), alpha * acc + _dot(vt, p.astype(bf16))


MOBA_HEADS_PER_STEP = 4


def _moba_prompt_kernel(slope_ref, q_ref, k_ref, vt_ref, o_ref, sel_ref):
    i = pl.program_id(2)
    mb, dh, hg = MOBA_BLOCK, HEAD_DIM, MOBA_HEADS_PER_STEP
    nb = k_ref.shape[0] // mb
    krow = lax.broadcasted_iota(jnp.int32, (mb, mb), 0)
    qcol = lax.broadcasted_iota(jnp.int32, (mb, mb), 1)
    rel = (qcol - krow).astype(f32)
    blk = lax.broadcasted_iota(jnp.int32, (nb, mb), 0)
    own = pl.multiple_of(i * mb, mb)
    heads = []
    for hh in range(hg):
        cs = slice(hh * dh, (hh + 1) * dh)
        q = q_ref[:, cs]
        slope = slope_ref[0, hh:hh + 1, :1]
        kmean = jnp.concatenate(
            [jnp.mean(k_ref[pl.ds(j * mb, mb), cs], axis=0, keepdims=True) for j in range(nb)], axis=0)
        gate = _dot_nt(kmean, q, HIGHEST)
        sel_ref[hh] = _topk_rows(jnp.where(blk < i, gate, NEG), MOBA_TOPK)
        qb = (q * SCALE).astype(bf16)
        bias = slope * rel
        s = _dot_nt(k_ref[pl.ds(own, mb), cs].astype(bf16), qb) - bias
        s = jnp.where(krow <= qcol, s, NEG)
        heads.append((cs, qb, bias, slope, _softmax_first(s, vt_ref[cs, pl.ds(own, mb)].astype(bf16))))

    def body(j, carry):
        off = pl.multiple_of(j * mb, mb)
        out = []
        for hh, (cs, qb, bias, slope, _) in enumerate(heads):
            s = _dot_nt(k_ref[pl.ds(off, mb), cs].astype(bf16), qb) - bias - slope * ((i - j) * mb).astype(f32)
            s = jnp.where(sel_ref[hh, pl.ds(j, 1), :] > 0.5, s, NEG)
            out.append(_softmax_next(s, vt_ref[cs, pl.ds(off, mb)].astype(bf16), *carry[hh]))
        return tuple(out)

    final = lax.fori_loop(0, i, body, tuple(h[4] for h in heads))
    for hh, (cs, *_rest) in enumerate(heads):
        _, l_i, acc = final[hh]
        o_ref[:, cs] = (acc / l_i).T


def moba_prompt(qm, kvm, vmt, batch, seq):
    nq = seq // MOBA_BLOCK
    hg = MOBA_HEADS_PER_STEP
    wide = hg * HEAD_DIM
    return pl.pallas_call(
        _moba_prompt_kernel,
        grid=(batch, MOBA_HEADS // hg, nq),
        in_specs=[pl.BlockSpec((1, hg, LANES), lambda b, h, i: (h, 0, 0)),
                  pl.BlockSpec((MOBA_BLOCK, wide), lambda b, h, i: (b * nq + i, h)),
                  pl.BlockSpec((seq, wide), lambda b, h, i: (b, h)),
                  pl.BlockSpec((wide, seq), lambda b, h, i: (h, b))],
        out_specs=pl.BlockSpec((MOBA_BLOCK, wide), lambda b, h, i: (b * nq + i, h)),
        out_shape=jax.ShapeDtypeStruct((batch * seq, MOBA_W), f32),
        scratch_shapes=[pltpu.VMEM((hg, seq // MOBA_BLOCK, MOBA_BLOCK), f32)],
        compiler_params=_cparams(("arbitrary", "arbitrary", "arbitrary")),
        name="moba_prompt",
    )(_alibi_groups(MOBA_HEADS // hg, hg), qm, kvm, vmt)


CMP_HALF = NSA_CMP_LEN // 2
N_CMP_ROWS = 128


def _compress_rows(xa, xb, w1_ref, w2_ref):
    half = CMP_HALF * HEAD_DIM
    y = _dot(xa, w1_ref[0, :half, :].astype(bf16))
    z = _dot(xb, w1_ref[0, half:, :].astype(bf16))
    parts = []
    for r in range(y.shape[0] // N_CMP_ROWS):
        zr = z[r * N_CMP_ROWS:(r + 1) * N_CMP_ROWS]
        parts.append(y[r * N_CMP_ROWS:(r + 1) * N_CMP_ROWS] + pltpu.roll(zr, N_CMP_ROWS - 1, 0))
    hid = _gelu(jnp.concatenate(parts, axis=0))
    return _dot(hid.astype(bf16), w2_ref[0].astype(bf16))


def _cmp_prompt_kernel(x0_ref, x1_ref, pe_ref, w1_ref, w2_ref, o_ref):
    pe = pe_ref[0]
    xa, xb = [], []
    for x_ref in (x0_ref, x1_ref):
        pa, pb = [], []
        for l in range(CMP_HALF):
            xl = x_ref[pl.ds(l, N_CMP_ROWS, stride=CMP_HALF), :]
            pa.append((xl + pe[l:l + 1]).astype(bf16))
            pb.append((xl + pe[CMP_HALF + l:CMP_HALF + l + 1]).astype(bf16))
        xa.append(jnp.concatenate(pa, axis=1))
        xb.append(jnp.concatenate(pb, axis=1))
    out = _compress_rows(jnp.concatenate(xa, axis=0), jnp.concatenate(xb, axis=0), w1_ref, w2_ref)
    for g in range(NSA_KV_HEADS):
        o_ref[0, 0, g] = out[g * N_CMP_ROWS:(g + 1) * N_CMP_ROWS]


def compress_prompt(kvc, pe2, w12, w22, batch, seq):
    return pl.pallas_call(
        _cmp_prompt_kernel,
        grid=(batch, 2),
        in_specs=[pl.BlockSpec((seq, HEAD_DIM), lambda b, kv: (b, NSA_KV_HEADS * kv)),
                  pl.BlockSpec((seq, HEAD_DIM), lambda b, kv: (b, NSA_KV_HEADS * kv + 1)),
                  pl.BlockSpec((1, NSA_CMP_LEN, HEAD_DIM), lambda b, kv: (kv, 0, 0)),
                  pl.BlockSpec((1, NSA_CMP_LEN * HEAD_DIM, NSA_CMP_HIDDEN), lambda b, kv: (kv, 0, 0)),
                  pl.BlockSpec((1, NSA_CMP_HIDDEN, HEAD_DIM), lambda b, kv: (kv, 0, 0))],
        out_specs=pl.BlockSpec((1, 1, NSA_KV_HEADS, N_CMP_ROWS, HEAD_DIM), lambda b, kv: (b, kv, 0, 0, 0)),
        out_shape=jax.ShapeDtypeStruct((batch, 2, NSA_KV_HEADS, N_CMP_ROWS, HEAD_DIM), f32),
        compiler_params=_cparams(("arbitrary", "arbitrary")),
        name="compress_prompt",
    )(kvc, kvc, pe2, w12, w22)


NSA_TQ = 128
N_CMP = 127


def _masked_softmax(s, mask):
    s = jnp.where(mask, s, NEG)
    m = jnp.max(s, axis=-1, keepdims=True)
    m = jnp.where(m > NEG, m, 0.0)
    p = jnp.where(mask, jnp.exp(s - m), 0.0)
    return p / jnp.maximum(jnp.sum(p, axis=-1, keepdims=True), 1e-30)


def _overlap_matrix():
    c = lax.broadcasted_iota(jnp.int32, (LANES, LANES), 0)
    j = lax.broadcasted_iota(jnp.int32, (LANES, LANES), 1)
    cs = NSA_CMP_STRIDE * c
    bs = NSA_SEL_BLOCK * j
    return ((cs < bs + NSA_SEL_BLOCK) & (cs + NSA_CMP_LEN - 1 >= bs)).astype(f32)


NSA_TK = 256
N_SEL_BLOCKS = 32


def _nsa_prompt_kernel(slope_ref, q_ref, gt_ref, ck_ref, cvt_ref, ks_ref, vst_ref, kw_ref, vwt_ref, o_ref, sel_ref):
    i = pl.program_id(1)
    tq, tk, R, G, ls, dh = NSA_TQ, NSA_TK, NSA_GROUP, NSA_KV_HEADS, NSA_SEL_BLOCK, HEAD_DIM
    W = R * tq
    q_all = q_ref[...]
    gates_t = gt_ref[...].T
    n_lane = lax.broadcasted_iota(jnp.int32, (1, W), 1) & (tq - 1)
    t_lane = i * tq + n_lane
    t_q = i * tq + lax.broadcasted_iota(jnp.int32, (N_SEL_BLOCKS, tq), 1)
    jrow = lax.broadcasted_iota(jnp.int32, (N_SEL_BLOCKS, tq), 0)
    crow = lax.broadcasted_iota(jnp.int32, (N_CMP_ROWS, W), 0)
    visible = (NSA_CMP_STRIDE * crow + NSA_CMP_LEN - 1 <= t_lane) & (crow < N_CMP)
    oj = lax.broadcasted_iota(jnp.int32, (N_SEL_BLOCKS, N_CMP_ROWS), 0) * ls
    oc = lax.broadcasted_iota(jnp.int32, (N_SEL_BLOCKS, N_CMP_ROWS), 1) * NSA_CMP_STRIDE
    overlap_t = ((oc < oj + ls) & (oc + NSA_CMP_LEN - 1 >= oj)).astype(f32)
    krow = lax.broadcasted_iota(jnp.int32, (tk, W), 0)
    rel_i = n_lane - krow
    rel = rel_i.astype(f32)
    diag = (i * tq) // tk
    diag_off = pl.multiple_of(diag * tk, tk)

    def chosen_rows(g, kt):
        parts = [jnp.broadcast_to(sel_ref[g, pl.ds(kt * (tk // ls) + b, 1), :], (ls, W)) for b in range(tk // ls)]
        return jnp.concatenate(parts, axis=0) > 0.5

    groups = []
    for g in range(G):
        gs = slice(g * dh, (g + 1) * dh)
        qs = jnp.concatenate([q_all[:, (g * R + r) * dh:(g * R + r + 1) * dh] for r in range(R)], axis=0)
        slope = slope_ref[g]
        s_c = jnp.where(visible, _dot_nt(ck_ref[0, g], qs, HIGHEST) * SCALE, NEG)
        m_c = jnp.max(s_c, axis=0, keepdims=True)
        p_c = jnp.where(visible, jnp.exp(s_c - jnp.where(m_c > NEG, m_c, 0.0)), 0.0)
        p_c = p_c / jnp.maximum(jnp.sum(p_c, axis=0, keepdims=True), 1e-30)
        o_c = _dot(cvt_ref[0, g].astype(bf16), p_c.astype(bf16))
        p_sum = p_c[:, 0:tq]
        for r in range(1, R):
            p_sum = p_sum + p_c[:, r * tq:(r + 1) * tq]
        imp = _dot(overlap_t, p_sum, HIGHEST)
        sel = _topk_rows(jnp.where(jrow < t_q // ls, imp, NEG), NSA_SEL_TOPK)
        sel_ref[g] = jnp.concatenate([sel] * R, axis=1)
        qb = (qs * SCALE).astype(bf16)
        bias = slope * rel
        d0 = slope * (i * tq - diag * tk).astype(f32)
        dist = rel_i + (i * tq - diag * tk)
        key_blk = (diag * tk + krow) // ls
        s = _dot_nt(ks_ref[pl.ds(diag_off, tk), gs].astype(bf16), qb) - bias - d0
        ok = chosen_rows(g, diag) | ((key_blk == t_lane // ls) & (dist >= 0))
        slc0 = _softmax_first(jnp.where(ok, s, NEG), vst_ref[gs, pl.ds(diag_off, tk)].astype(bf16))
        s = _dot_nt(kw_ref[pl.ds(diag_off, tk), gs].astype(bf16), qb) - bias - d0
        win0 = _softmax_first(jnp.where((dist >= 0) & (dist < NSA_WINDOW), s, NEG),
                              vwt_ref[gs, pl.ds(diag_off, tk)].astype(bf16))
        groups.append((gs, qb, bias, slope, o_c, slc0, win0))

    def slc_body(kt, carry):
        off = pl.multiple_of(kt * tk, tk)
        out = []
        for g, (gs, qb, bias, slope, *_rest) in enumerate(groups):
            s = _dot_nt(ks_ref[pl.ds(off, tk), gs].astype(bf16), qb) - bias - slope * (i * tq - kt * tk).astype(f32)
            s = jnp.where(chosen_rows(g, kt), s, NEG)
            out.append(_softmax_next(s, vst_ref[gs, pl.ds(off, tk)].astype(bf16), *carry[g]))
        return tuple(out)

    def win_body(kt, carry):
        off = pl.multiple_of(kt * tk, tk)
        shift = i * tq - kt * tk
        out = []
        for g, (gs, qb, bias, slope, *_rest) in enumerate(groups):
            s = _dot_nt(kw_ref[pl.ds(off, tk), gs].astype(bf16), qb) - bias - slope * shift.astype(f32)
            s = jnp.where(rel_i + shift < NSA_WINDOW, s, NEG)
            out.append(_softmax_next(s, vwt_ref[gs, pl.ds(off, tk)].astype(bf16), *carry[g]))
        return tuple(out)

    slc = lax.fori_loop(0, diag, slc_body, tuple(grp[5] for grp in groups))
    first_win = jnp.maximum(i * tq - (NSA_WINDOW - 1), 0) // tk
    win = lax.fori_loop(first_win, diag, win_body, tuple(grp[6] for grp in groups))

    for g, (gs, qb, bias, slope, o_c, *_rest) in enumerate(groups):
        def gate_row(branch):
            return jnp.concatenate([gates_t[NSA_BRANCHES * (g * R + r) + branch:NSA_BRANCHES * (g * R + r) + branch + 1]
                                    for r in range(R)], axis=1)
        o_t = (gate_row(0) * o_c + gate_row(1) * (slc[g][2] / slc[g][1]) + gate_row(2) * (win[g][2] / win[g][1]))
        o = o_t.T
        for r in range(R):
            o_ref[:, (g * R + r) * dh:(g * R + r + 1) * dh] = o[r * tq:(r + 1) * tq]


def _alibi_groups(n_groups, group):
    n = n_groups * group
    s = (2.0 ** (-8.0 * np.arange(1, n + 1) / n)).reshape(n_groups, group)
    return jnp.asarray(np.broadcast_to(s[:, :, None], (n_groups, group, LANES)), dtype=f32)


def nsa_prompt(qn, gates, ck, cvt, kvs, vst, kvw, vwt, batch, seq):
    assert seq // NSA_SEL_BLOCK == N_SEL_BLOCKS and (seq - NSA_CMP_LEN) // NSA_CMP_STRIDE + 1 == N_CMP
    nq = seq // NSA_TQ
    G, R = NSA_KV_HEADS, NSA_GROUP
    n = G * R
    slopes = (2.0 ** (-8.0 * np.arange(1, n + 1) / n)).reshape(G, 1, R, 1)
    slope_lanes = jnp.asarray(np.broadcast_to(slopes, (G, 1, R, NSA_TQ)).reshape(G, 1, R * NSA_TQ), dtype=f32)
    full = lambda a: pl.BlockSpec(a.shape, lambda b, i: (0,) * a.ndim)
    return pl.pallas_call(
        _nsa_prompt_kernel,
        grid=(batch, nq),
        in_specs=[full(slope_lanes),
                  pl.BlockSpec((NSA_TQ, NSA_W), lambda b, i: (b * nq + i, 0)),
                  pl.BlockSpec((NSA_TQ, LANES), lambda b, i: (b * nq + i, 0)),
                  pl.BlockSpec((1, G, N_CMP_ROWS, HEAD_DIM), lambda b, i: (b, 0, 0, 0)),
                  pl.BlockSpec((1, G, HEAD_DIM, N_CMP_ROWS), lambda b, i: (b, 0, 0, 0)),
                  pl.BlockSpec((seq, NSA_KVW), lambda b, i: (b, 0)),
                  pl.BlockSpec((NSA_KVW, seq), lambda b, i: (0, b)),
                  pl.BlockSpec((seq, NSA_KVW), lambda b, i: (b, 0)),
                  pl.BlockSpec((NSA_KVW, seq), lambda b, i: (0, b))],
        out_specs=pl.BlockSpec((NSA_TQ, NSA_W), lambda b, i: (b * nq + i, 0)),
        out_shape=jax.ShapeDtypeStruct((batch * seq, NSA_W), f32),
        scratch_shapes=[pltpu.VMEM((G, N_SEL_BLOCKS, R * NSA_TQ), f32)],
        compiler_params=_cparams(("arbitrary", "arbitrary")),
        name="nsa_prompt",
    )(slope_lanes, qn, gates, ck, cvt, kvs, vst, kvw, vwt)


def _head_slopes(n):
    s = 2.0 ** (-8.0 * np.arange(1, n + 1) / n)
    return jnp.asarray(np.broadcast_to(s[:, None], (n, LANES)), dtype=f32)


def _moba_sample_kernel(pt_ref, slope_ref, q_ref, kvn_ref, *refs):
    page_refs, o_ref = refs[:-1], refs[-1]
    n_pages = len(page_refs)
    mb = MOBA_BLOCK
    ppb = mb // PAGE_SIZE
    nb = n_pages // ppb
    t_new = n_pages * PAGE_SIZE
    q = q_ref[0]
    slope = slope_ref[...]
    tok = lax.broadcasted_iota(jnp.int32, (mb, MOBA_HEADS, LANES), 0)
    ones = jnp.ones((HEAD_DIM, LANES), bf16)
    gates, ms, ls, os_ = [], [], [], []
    for j in range(nb):
        k = jnp.concatenate([page_refs[ppb * j + u][pl.ds(0, PAGE_SIZE, stride=2)] for u in range(ppb)], axis=0)
        v = jnp.concatenate([page_refs[ppb * j + u][pl.ds(1, PAGE_SIZE, stride=2)] for u in range(ppb)], axis=0)
        kmean = jnp.sum(k, axis=0) / mb
        gates.append(jnp.sum(q * kmean, axis=-1, keepdims=True))
        dist = (t_new - j * mb - tok).astype(f32)
        kq = (k * q[None]).reshape(mb * MOBA_HEADS, HEAD_DIM).astype(bf16)
        s = _dot(kq, ones).reshape(mb, MOBA_HEADS, LANES) * SCALE - slope[None] * dist
        m = jnp.max(s, axis=0)
        p = jnp.exp(s - m[None])
        ms.append(m)
        ls.append(jnp.sum(p, axis=0))
        os_.append(jnp.sum(p * v, axis=0))
    chosen = []
    for j in range(nb):
        rank = jnp.zeros_like(gates[j])
        for j2 in range(nb):
            if j2 != j:
                ahead = (gates[j2] >= gates[j]) if j2 < j else (gates[j2] > gates[j])
                rank = rank + ahead.astype(f32)
        chosen.append(rank < MOBA_TOPK)
    kn, vn = kvn_ref[0, 0], kvn_ref[0, 1]
    s_own = jnp.sum(q * kn, axis=-1, keepdims=True) * SCALE
    m_all = s_own
    for j in range(nb):
        m_all = jnp.maximum(m_all, jnp.where(chosen[j], ms[j], NEG))
    w_own = jnp.exp(s_own - m_all)
    l_all = w_own
    o_all = w_own * vn
    for j in range(nb):
        w = jnp.where(chosen[j], jnp.exp(ms[j] - m_all), 0.0)
        l_all = l_all + w * ls[j]
        o_all = o_all + w * os_[j]
    o_ref[0] = o_all / l_all


def moba_sample(q3, kvn4, cache3, page_table):
    nseq, n_pages = page_table.shape
    rows = 2 * PAGE_SIZE
    page_spec = lambda p: pl.BlockSpec((rows, MOBA_HEADS, HEAD_DIM), lambda b, pt: (pt[b, p], 0, 0))
    return pl.pallas_call(
        _moba_sample_kernel,
        grid_spec=pltpu.PrefetchScalarGridSpec(
            num_scalar_prefetch=1, grid=(nseq,),
            in_specs=[pl.BlockSpec((MOBA_HEADS, LANES), lambda b, pt: (0, 0)),
                      pl.BlockSpec((1, MOBA_HEADS, HEAD_DIM), lambda b, pt: (b, 0, 0)),
                      pl.BlockSpec((1, 2, MOBA_HEADS, HEAD_DIM), lambda b, pt: (b, 0, 0, 0))]
                     + [page_spec(p) for p in range(n_pages)],
            out_specs=pl.BlockSpec((1, MOBA_HEADS, HEAD_DIM), lambda b, pt: (b, 0, 0))),
        out_shape=jax.ShapeDtypeStruct((nseq, MOBA_HEADS, HEAD_DIM), f32),
        compiler_params=_cparams(("arbitrary",)),
        name="moba_sample",
    )(page_table, _head_slopes(MOBA_HEADS), q3, kvn4, *([cache3] * n_pages))


KV_ROWS = 2 * NSA_KV_HEADS


COMPRESS_BATCH = 2


def _nsa_sample_compress_kernel(pt_ref, pe_ref, w1k_ref, w1v_ref, w2k_ref, w2v_ref, *refs):
    page_refs, ckv_ref = refs[:-1], refs[-1]
    n_pages = len(page_refs) // COMPRESS_BATCH
    per_page = PAGE_SIZE // CMP_HALF
    G = NSA_KV_HEADS
    for kv, (w1_ref, w2_ref) in enumerate(((w1k_ref, w2k_ref), (w1v_ref, w2v_ref))):
        pe = pe_ref[kv]
        xa, xb = [], []
        for s in range(COMPRESS_BATCH):
            pages = page_refs[s * n_pages:(s + 1) * n_pages]
            for g in range(G):
                pa, pb = [], []
                for l in range(CMP_HALF):
                    xl = jnp.concatenate(
                        [pr[pl.ds(KV_ROWS * l + G * kv + g, per_page, stride=KV_ROWS * CMP_HALF), :] for pr in pages], axis=0)
                    pa.append((xl + pe[l:l + 1]).astype(bf16))
                    pb.append((xl + pe[CMP_HALF + l:CMP_HALF + l + 1]).astype(bf16))
                xa.append(jnp.concatenate(pa, axis=1))
                xb.append(jnp.concatenate(pb, axis=1))
        comp = _compress_rows(jnp.concatenate(xa, axis=0), jnp.concatenate(xb, axis=0), w1_ref, w2_ref)
        for s in range(COMPRESS_BATCH):
            for g in range(G):
                ckv_ref[s, kv, g] = comp[(s * G + g) * N_CMP_ROWS:(s * G + g + 1) * N_CMP_ROWS]


SELECT_BATCH = 8


def _nsa_sample_select_kernel(q_ref, ckv_ref, oc_ref, sel_ref, *, t_new):
    sb = q_ref.shape[0]
    G, R, H = NSA_KV_HEADS, NSA_GROUP, NSA_HEADS
    row = lax.broadcasted_iota(jnp.int32, (H, LANES), 0)
    lane = lax.broadcasted_iota(jnp.int32, (sb * H, LANES), 1)
    n_cmp = (t_new + 1 - NSA_CMP_LEN) // NSA_CMP_STRIDE + 1
    visible = (lane < n_cmp) & (NSA_CMP_STRIDE * lane + NSA_CMP_LEN - 1 <= t_new)
    first = row < R
    s_c = jnp.concatenate(
        [jnp.where(first, _dot_nt(q_ref[b], ckv_ref[b, 0, 0], HIGHEST), _dot_nt(q_ref[b], ckv_ref[b, 0, 1], HIGHEST))
         for b in range(sb)], axis=0) * SCALE
    p_c = _masked_softmax(s_c, visible)
    p_b = p_c.astype(bf16)
    for b in range(sb):
        pb = p_b[b * H:(b + 1) * H]
        oc_ref[b] = jnp.where(first, _dot(pb, ckv_ref[b, 1, 0].astype(bf16)), _dot(pb, ckv_ref[b, 1, 1].astype(bf16)))
    gi = lax.broadcasted_iota(jnp.int32, (sb * G, sb * H), 0)
    gj = lax.broadcasted_iota(jnp.int32, (sb * G, sb * H), 1)
    p_sum = _dot((gj // R == gi).astype(f32), p_c, HIGHEST)
    imp = _dot(p_sum, _overlap_matrix(), HIGHEST)
    lane_g = lax.broadcasted_iota(jnp.int32, (sb * G, LANES), 1)
    cur = jnp.where(lane_g < t_new // NSA_SEL_BLOCK, imp, NEG)
    out = jnp.zeros((sb * G, LANES), jnp.int32)
    for s in range(NSA_SEL_TOPK):
        m = jnp.max(cur, axis=-1, keepdims=True)
        idx = jnp.min(jnp.where((cur == m) & (m > NEG), lane_g, LANES), axis=-1, keepdims=True)
        found = idx < LANES
        out = jnp.where(lane_g == s, jnp.where(found, idx, 0), out)
        out = jnp.where(lane_g == NSA_SEL_TOPK + s, found.astype(jnp.int32), out)
        cur = jnp.where(lane_g == idx, NEG, cur)
    sel_ref[...] = out


def nsa_sample_cmp(q3, cache2, page_table, pe2, w1k, w1v, w2k, w2v):
    nseq, n_pages = page_table.shape
    G = NSA_KV_HEADS
    rows = PAGE_SIZE * KV_ROWS
    cb = COMPRESS_BATCH
    page_spec = lambda s, p: pl.BlockSpec((rows, HEAD_DIM), lambda b, pt: (pt[cb * b + s, p], 0))
    full = lambda a: pl.BlockSpec(a.shape, lambda b, pt: (0,) * a.ndim)
    ckv = pl.pallas_call(
        _nsa_sample_compress_kernel,
        grid_spec=pltpu.PrefetchScalarGridSpec(
            num_scalar_prefetch=1, grid=(nseq // cb,),
            in_specs=[full(pe2), full(w1k), full(w1v), full(w2k), full(w2v)]
                     + [page_spec(s, p) for s in range(cb) for p in range(n_pages)],
            out_specs=pl.BlockSpec((cb, 2, G, N_CMP_ROWS, HEAD_DIM), lambda b, pt: (b, 0, 0, 0, 0))),
        out_shape=jax.ShapeDtypeStruct((nseq, 2, G, N_CMP_ROWS, HEAD_DIM), f32),
        compiler_params=_cparams(("arbitrary",)),
        name="nsa_sample_compress",
    )(page_table, pe2, w1k, w1v, w2k, w2v, *([cache2] * (cb * n_pages)))
    sb = SELECT_BATCH
    return pl.pallas_call(
        functools.partial(_nsa_sample_select_kernel, t_new=n_pages * PAGE_SIZE),
        grid=(nseq // sb,),
        in_specs=[pl.BlockSpec((sb, NSA_HEADS, HEAD_DIM), lambda i: (i, 0, 0)),
                  pl.BlockSpec((sb, 2, G, N_CMP_ROWS, HEAD_DIM), lambda i: (i, 0, 0, 0, 0))],
        out_specs=[pl.BlockSpec((sb, NSA_HEADS, HEAD_DIM), lambda i: (i, 0, 0)),
                   pl.BlockSpec((sb * G, LANES), lambda i: (i, 0))],
        out_shape=[jax.ShapeDtypeStruct((nseq, NSA_HEADS, HEAD_DIM), f32),
                   jax.ShapeDtypeStruct((nseq * G, LANES), jnp.int32)],
        compiler_params=_cparams(("arbitrary",)),
        name="nsa_sample_select",
    )(q3, ckv)


def _decode_attend(q, slope, keys, vals, pos, valid, k_own, v_own, t_new):
    s = _dot_nt(q.astype(bf16), keys.astype(bf16)) * SCALE - slope * (t_new - pos).astype(f32)
    s = jnp.where(valid, s, NEG)
    s_own = jnp.sum(q * k_own, axis=-1, keepdims=True) * SCALE
    m = jnp.maximum(jnp.max(s, axis=-1, keepdims=True), s_own)
    p = jnp.exp(s - m)
    p_own = jnp.exp(s_own - m)
    denom = jnp.sum(p, axis=-1, keepdims=True) + p_own
    return (_dot(p.astype(bf16), vals.astype(bf16)) + p_own * v_own) / denom


def _nsa_sample_attn_kernel(pt_ref, sel_ref, slope_ref, q_ref, gt_ref, oc_ref, ksn_ref, kwn_ref, win_ref, *refs):
    blk_refs, (o_ref, wout_ref) = refs[:-2], refs[-2:]
    b = pl.program_id(0)
    G, R, ls, K = NSA_KV_HEADS, NSA_GROUP, NSA_SEL_BLOCK, NSA_SEL_TOPK
    t_new = pt_ref.shape[1] * PAGE_SIZE
    q = q_ref[0]
    slope = slope_ref[:, :1]
    row = lax.broadcasted_iota(jnp.int32, (G * R, HEAD_DIM), 0)
    own_rows = lambda ref, kv: jnp.where(row < R, ref[0, G * kv:G * kv + 1], ref[0, G * kv + 1:G * kv + 2])
    lane_s = lax.broadcasted_iota(jnp.int32, (1, K * ls), 1)
    n_win = win_ref.shape[0] // KV_ROWS
    lane_w = lax.broadcasted_iota(jnp.int32, (1, n_win), 1)
    pos_w = t_new - n_win + lane_w
    o_s = jnp.zeros((G * R, HEAD_DIM), f32)
    o_w = jnp.zeros((G * R, HEAD_DIM), f32)
    for g in range(G):
        keys = jnp.concatenate([blk_refs[g * K + s][pl.ds(g, ls, stride=KV_ROWS), :] for s in range(K)], axis=0)
        vals = jnp.concatenate([blk_refs[g * K + s][pl.ds(G + g, ls, stride=KV_ROWS), :] for s in range(K)], axis=0)
        pos = jnp.zeros((1, K * ls), jnp.int32)
        valid = jnp.zeros((1, K * ls), jnp.int32)
        for s in range(K):
            here = lane_s // ls == s
            pos = jnp.where(here, sel_ref[b, g * 2 * K + s] * ls + lane_s - s * ls, pos)
            valid = jnp.where(here, sel_ref[b, g * 2 * K + K + s], valid)
        mine = (row >= g * R) & (row < (g + 1) * R)
        o_s = jnp.where(mine, _decode_attend(q, slope, keys, vals, pos, valid > 0, own_rows(ksn_ref, 0), own_rows(ksn_ref, 1), t_new), o_s)
        keys_w = win_ref[pl.ds(g, n_win, stride=KV_ROWS), :]
        vals_w = win_ref[pl.ds(G + g, n_win, stride=KV_ROWS), :]
        ok_w = (pos_w > t_new - NSA_WINDOW) & (pos_w >= 0)
        o_w = jnp.where(mine, _decode_attend(q, slope, keys_w, vals_w, pos_w, ok_w, own_rows(kwn_ref, 0), own_rows(kwn_ref, 1), t_new), o_w)
    gt = gt_ref[0]
    o_ref[0] = gt[:, 0:1] * oc_ref[0] + gt[:, 1:2] * o_s + gt[:, 2:3] * o_w
    total = win_ref.shape[0]
    shifted = pltpu.roll(win_ref[...], total - KV_ROWS, 0)
    new8 = jnp.concatenate([kwn_ref[0], kwn_ref[0]], axis=0)
    row8 = lax.broadcasted_iota(jnp.int32, (2 * KV_ROWS, HEAD_DIM), 0)
    wout_ref[pl.ds(0, total - 2 * KV_ROWS), :] = shifted[:total - 2 * KV_ROWS]
    wout_ref[pl.ds(total - 2 * KV_ROWS, 2 * KV_ROWS), :] = jnp.where(row8 >= KV_ROWS, new8, shifted[total - 2 * KV_ROWS:])


def nsa_sample_attn(q3, gates3, o_c, ksn, kwn, slc2, win2, page_table, sel_flat):
    nseq = page_table.shape[0]
    G, K, ls = NSA_KV_HEADS, NSA_SEL_TOPK, NSA_SEL_BLOCK
    blocks_per_page = PAGE_SIZE // ls
    n_blocks = page_table.shape[1] * blocks_per_page
    win_rows = win2.shape[0] // nseq

    def blk_spec(g, s):
        def index(b, pt, sel):
            bb = jnp.minimum(b, nseq - 1)
            blk = jnp.clip(sel[bb, g * 2 * K + s], 0, n_blocks - 1)
            return (pt[bb, blk // blocks_per_page] * blocks_per_page + blk % blocks_per_page, 0)
        return pl.BlockSpec((ls * KV_ROWS, HEAD_DIM), index)

    per_seq = lambda shape: pl.BlockSpec((1,) + shape, lambda b, pt, sel: (b,) + (0,) * len(shape))
    return pl.pallas_call(
        _nsa_sample_attn_kernel,
        grid_spec=pltpu.PrefetchScalarGridSpec(
            num_scalar_prefetch=2, grid=(nseq,),
            in_specs=[pl.BlockSpec((NSA_HEADS, LANES), lambda b, pt, sel: (0, 0)),
                      per_seq((NSA_HEADS, HEAD_DIM)), per_seq((NSA_HEADS, LANES)), per_seq((NSA_HEADS, HEAD_DIM)),
                      per_seq((KV_ROWS, HEAD_DIM)), per_seq((KV_ROWS, HEAD_DIM)),
                      pl.BlockSpec((win_rows, HEAD_DIM), lambda b, pt, sel: (b, 0))]
                     + [blk_spec(g, s) for g in range(G) for s in range(K)],
            out_specs=[per_seq((NSA_HEADS, HEAD_DIM)),
                       pl.BlockSpec((win_rows, HEAD_DIM), lambda b, pt, sel: (b, 0))]),
        out_shape=[jax.ShapeDtypeStruct((nseq, NSA_HEADS, HEAD_DIM), f32),
                   jax.ShapeDtypeStruct(win2.shape, f32)],
        compiler_params=_cparams(("arbitrary",)),
        name="nsa_sample_attn",
    )(page_table, sel_flat, _head_slopes(NSA_HEADS), q3, gates3, o_c, ksn, kwn, win2, *([slc2] * (G * K)))


def _mid_kernel(x_ref, om_ref, on_ref, w_ref, g_ref, gate_ref, sh_ref, sc_ref, x1_ref, h2_ref):
    proj = (_dot(om_ref[...].astype(bf16), w_ref[:MOBA_W, :]) + _dot(on_ref[...].astype(bf16), w_ref[MOBA_W:, :]))
    x1 = x_ref[...] + gate_ref[0] * proj
    x1_ref[...] = x1
    h2_ref[...] = _rms_mod(x1, g_ref[...], sh_ref[0], sc_ref[0]).astype(bf16)


def mid_block(x, o_m, o_n, w_out_b, g_norm2, mods3, tm, rows_per_mod):
    t = x.shape[0]
    r = mods3.shape[1]
    tiles_per_mod = rows_per_mod // tm
    mod_spec = lambda which: pl.BlockSpec((1, r, D_MODEL), lambda i: (i // tiles_per_mod, 0, which))
    return pl.pallas_call(
        _mid_kernel,
        grid=(t // tm,),
        in_specs=[pl.BlockSpec((tm, D_MODEL), lambda i: (i, 0)),
                  pl.BlockSpec((tm, MOBA_W), lambda i: (i, 0)),
                  pl.BlockSpec((tm, NSA_W), lambda i: (i, 0)),
                  pl.BlockSpec((MOBA_W + NSA_W, D_MODEL), lambda i: (0, 0)),
                  pl.BlockSpec((1, D_MODEL), lambda i: (0, 0)),
                  mod_spec(2), mod_spec(3), mod_spec(4)],
        out_specs=[pl.BlockSpec((tm, D_MODEL), lambda i: (i, 0)),
                   pl.BlockSpec((tm, D_MODEL), lambda i: (i, 0))],
        out_shape=[jax.ShapeDtypeStruct((t, D_MODEL), f32), jax.ShapeDtypeStruct((t, D_MODEL), bf16)],
        compiler_params=_cparams(("arbitrary",)),
        name="mid_block",
    )(x, o_m, o_n, w_out_b, g_norm2.reshape(1, D_MODEL), mods3, mods3, mods3)


PEER_HALF = PEER_QDIM // 2
PEER_A_FULL = 8


def _top_values(s, k):
    tops = []
    cur = s
    for _ in range(k):
        m = jnp.max(cur, axis=0, keepdims=True)
        tops.append(m)
        cur = jnp.where(cur == m, NEG, cur)
    return jnp.concatenate(tops, axis=0)


def _peer_route_kernel(h_ref, wq_ref, keys_ref, s1_ref, thr_ref, e1_ref, coef_ref):
    q = _dot(h_ref[...], wq_ref[...])
    k = PEER_TOPK
    for h in range(PEER_HEADS):
        base = h * PEER_QDIM
        s0 = _dot_nt(keys_ref[h, 0], q[:, base:base + PEER_HALF], HIGHEST)
        s1 = _dot_nt(keys_ref[h, 1], q[:, base + PEER_HALF:base + PEER_QDIM], HIGHEST)
        top0 = _top_values(s0, k)
        top1 = _top_values(s1, k)
        cand = jnp.concatenate([top0[a:a + 1] + top1 for a in range(PEER_A_FULL)]
                               + [top0[PEER_A_FULL:] + top1[0:1]], axis=0)
        best = _top_values(cand, k)
        tau = best[k - 1:k]
        z = jnp.sum(jnp.exp(best - best[0:1]), axis=0, keepdims=True)
        thr = jnp.full(s0.shape, jnp.inf, f32)
        for a in range(k):
            thr_a = jnp.min(jnp.where(top0[a:a + 1] + top1 >= tau, top1, jnp.inf), axis=0, keepdims=True)
            thr = jnp.where(s0 == top0[a:a + 1], thr_a, thr)
        s1_ref[h] = s1
        thr_ref[h] = thr
        e1_ref[h] = jnp.exp(s1 - top1[0:1])
        coef_ref[h] = jnp.exp(s0 - top0[0:1]) / z


def peer_route(h2, wq_b, keys, tm):
    t = h2.shape[0]
    out_spec = pl.BlockSpec((PEER_HEADS, PEER_KEYS, tm), lambda i: (0, 0, i))
    out_shape = jax.ShapeDtypeStruct((PEER_HEADS, PEER_KEYS, t), f32)
    return pl.pallas_call(
        _peer_route_kernel,
        grid=(t // tm,),
        in_specs=[pl.BlockSpec((tm, D_MODEL), lambda i: (i, 0)),
                  pl.BlockSpec(wq_b.shape, lambda i: (0, 0)),
                  pl.BlockSpec(keys.shape, lambda i: (0, 0, 0, 0))],
        out_specs=[out_spec] * 4,
        out_shape=[out_shape] * 4,
        compiler_params=_cparams(("arbitrary",)),
        name="peer_route",
    )(h2, wq_b, keys)


PEER_EXPERT_TILE = 1024
PEER_SLAB = 512


def _peer_expert_kernel(ht_ref, u_ref, vt_ref, s1_ref, thr_ref, e1_ref, coef_ref, x1_ref, gate_ref, gf_ref,
                        y_ref, acc_ref, p_ref):
    e = pl.program_id(1)
    te = u_ref.shape[0]
    rows_per_step = te // PEER_KEYS

    @pl.when(e == 0)
    def _():
        acc_ref[...] = jnp.zeros_like(acc_ref)

    tm = ht_ref.shape[1]
    for slab in range(te // PEER_SLAB):
        act = _gelu(_dot(u_ref[slab * PEER_SLAB:(slab + 1) * PEER_SLAB, :], ht_ref[...]))
        for rr in range(PEER_SLAB // PEER_KEYS):
            r = slab * (PEER_SLAB // PEER_KEYS) + rr
            i0 = e * rows_per_step + r
            for c in range(tm // LANES):
                cs = slice(c * LANES, (c + 1) * LANES)
                w = jnp.zeros((PEER_KEYS, LANES), f32)
                for h in range(PEER_HEADS):
                    thr = thr_ref[h, pl.ds(i0, 1), :][:, cs]
                    coef = coef_ref[h, pl.ds(i0, 1), :][:, cs]
                    w = w + jnp.where(s1_ref[h, :, cs] >= thr, e1_ref[h, :, cs] * coef, 0.0)
                p_ref[r * PEER_KEYS:(r + 1) * PEER_KEYS, cs] = (
                    w * act[rr * PEER_KEYS:(rr + 1) * PEER_KEYS, cs]).astype(bf16)
    acc_ref[...] += _dot(vt_ref[...], p_ref[...])

    @pl.when(e == pl.num_programs(1) - 1)
    def _():
        y = x1_ref[...] + gate_ref[0] * acc_ref[...].T
        y_ref[...] = (y * lax.rsqrt(jnp.mean(y * y, axis=-1, keepdims=True) + RMS_EPS)) * gf_ref[...]


def peer_experts(h2t, u_b, vt_b, route, x1, mods3, g_final, tm, te, rows_per_mod):
    t = h2t.shape[1]
    r = mods3.shape[1]
    n_exp = u_b.shape[0]
    tiles_per_mod = rows_per_mod // tm
    once = dict(pipeline_mode=pl.Buffered(1))
    route_spec = pl.BlockSpec((PEER_HEADS, PEER_KEYS, tm), lambda i, e: (0, 0, i), **once)
    return pl.pallas_call(
        _peer_expert_kernel,
        grid=(t // tm, n_exp // te),
        in_specs=[pl.BlockSpec((D_MODEL, tm), lambda i, e: (0, i), **once),
                  pl.BlockSpec((te, D_MODEL), lambda i, e: (e, 0)),
                  pl.BlockSpec((D_MODEL, te), lambda i, e: (0, e)),
                  route_spec, route_spec, route_spec, route_spec,
                  pl.BlockSpec((tm, D_MODEL), lambda i, e: (i, 0), **once),
                  pl.BlockSpec((1, r, D_MODEL), lambda i, e: (i // tiles_per_mod, 0, 5)),
                  pl.BlockSpec((1, D_MODEL), lambda i, e: (0, 0))],
        out_specs=pl.BlockSpec((tm, D_MODEL), lambda i, e: (i, 0)),
        out_shape=jax.ShapeDtypeStruct((t, D_MODEL), f32),
        scratch_shapes=[pltpu.VMEM((D_MODEL, tm), f32), pltpu.VMEM((te, tm), bf16)],
        compiler_params=_cparams(("arbitrary", "arbitrary")),
        name="peer_experts",
    )(h2t, u_b, vt_b, *route, x1, mods3, g_final.reshape(1, D_MODEL))


def _group_forward(x2, mods3, tm, rows_per_mod, w, attend):
    proj = in_projection(x2, mods3, w["g_norm1"], w["w_in"], tm, rows_per_mod)
    o_m, o_n = attend(proj)
    x1, h2 = mid_block(x2, o_m, o_n, w["w_out"], w["g_norm2"], mods3, min(tm, 256), rows_per_mod)
    route = peer_route(h2, w["peer_w_q"], w["peer_keys"], min(tm, 256))
    y = peer_experts(h2.T, w["peer_u"], w["peer_vt"], route, x1, mods3, w["g_final"], tm, PEER_EXPERT_TILE, rows_per_mod)
    return proj, y


def kernel(x_prompt, x_sample, cache_moba_kv, cache_nsa_cmp_kv, cache_nsa_slc_kv, state_nsa_win_kv, page_table,
           c_prompt, c_sample, w_ada, b_ada, g_norm1, w_in, cmp_pe_k, cmp_w1_k, cmp_w2_k, cmp_pe_v, cmp_w1_v,
           cmp_w2_v, w_out, g_norm2, peer_w_q, peer_keys, peer_u, peer_v, g_final):
    assert w_ada.shape[0] == 1, "single layer"
    batch, seq, _ = x_prompt.shape
    nseq, dec_seq, _ = x_sample.shape
    assert dec_seq == 1 and state_nsa_win_kv.shape[2] == NSA_WINDOW and seq >= NSA_WINDOW
    G, H, dh = NSA_KV_HEADS, MOBA_HEADS, HEAD_DIM

    c_all = jnp.concatenate([c_prompt, c_sample], axis=0)
    pad = (-c_all.shape[0]) % 8
    mods = ada_mods(jnp.pad(c_all, ((0, pad), (0, 0))), w_ada[0], b_ada[0])
    mods_p = mods[:batch].reshape(batch, 1, N_MOD * D_MODEL)
    mods_s = mods[batch:batch + nseq].reshape(1, nseq, N_MOD * D_MODEL)

    p_in = w_in.shape[2]
    pe2 = jnp.stack([cmp_pe_k[0], cmp_pe_v[0]])
    w12 = jnp.stack([cmp_w1_k[0], cmp_w1_v[0]]).astype(bf16)
    w22 = jnp.stack([cmp_w2_k[0], cmp_w2_v[0]]).astype(bf16)
    w = dict(
        g_norm1=g_norm1[0], g_norm2=g_norm2[0], g_final=g_final,
        w_in=jnp.pad(w_in[0], ((0, 0), (0, IN_COLS - p_in))).astype(bf16),
        w_out=w_out[0].astype(bf16),
        peer_w_q=peer_w_q[0].astype(bf16), peer_keys=peer_keys[0],
        peer_u=peer_u[0].astype(bf16), peer_vt=peer_v[0].T.astype(bf16),
    )

    def attend_prompt(proj):
        qm, kvm, qn, kvc, kvs, kvw, gates = proj[:7]
        ckv = compress_prompt(kvc, pe2, w12, w22, batch, seq)
        o_m = moba_prompt(qm, kvm, kvm[:, MOBA_W:].T, batch, seq)
        o_n = nsa_prompt(qn, gates, ckv[:, 0], jnp.swapaxes(ckv[:, 1], 2, 3),
                         kvs, kvs[:, NSA_KVW:].T, kvw, kvw[:, NSA_KVW:].T, batch, seq)
        return o_m, o_n

    win_out = []

    def attend_sample(proj):
        qm, kvm, qn, kvc, kvs, kvw, gates = proj[:7]
        o_m = moba_sample(qm.reshape(nseq, H, dh), kvm.reshape(nseq, 2, H, dh), cache_moba_kv.reshape(-1, H, dh), page_table)
        qn3 = qn.reshape(nseq, NSA_HEADS, dh)
        o_c, sel = nsa_sample_cmp(qn3, cache_nsa_cmp_kv.reshape(-1, dh), page_table, pe2,
                                  w12[0:1], w12[1:2], w22[0:1], w22[1:2])
        sel_flat = sel[:, :2 * NSA_SEL_TOPK].reshape(nseq, G * 2 * NSA_SEL_TOPK)
        gates3 = jnp.pad(gates[:, :NSA_HEADS * NSA_BRANCHES].reshape(nseq, NSA_HEADS, NSA_BRANCHES),
                         ((0, 0), (0, 0), (0, LANES - NSA_BRANCHES)))
        o_n, wout = nsa_sample_attn(qn3, gates3, o_c, kvs.reshape(nseq, KV_ROWS, dh), kvw.reshape(nseq, KV_ROWS, dh),
                                    cache_nsa_slc_kv.reshape(-1, dh), state_nsa_win_kv.reshape(-1, dh), page_table, sel_flat)
        win_out.append(wout)
        return o_m.reshape(nseq, MOBA_W), o_n.reshape(nseq, NSA_W)

    proj_p, y_p = _group_forward(x_prompt.reshape(batch * seq, D_MODEL), mods_p, 512, seq, w, attend_prompt)
    proj_s, y_s = _group_forward(x_sample.reshape(nseq, D_MODEL), mods_s, nseq, nseq, w, attend_sample)

    kv_p = lambda a, nh: a.reshape(1, batch, seq, 2, nh, dh)
    kv_s = lambda a, nh: a.reshape(1, nseq, 1, 2, nh, dh)
    return (y_p.reshape(batch, seq, D_MODEL), y_s.reshape(nseq, 1, D_MODEL),
            kv_p(proj_p[7], H), kv_s(proj_s[7], H),
            kv_p(proj_p[8], G), kv_s(proj_s[8], G),
            kv_p(proj_p[9], G), kv_s(proj_s[9], G),
            kv_p(proj_p[10], G)[:, :, seq - NSA_WINDOW:],
            win_out[0].reshape(state_nsa_win_kv.shape))
```

```python
import functools

import jax
import jax.numpy as jnp
import numpy as np
from jax import lax
from jax.experimental import pallas as pl
from jax.experimental.pallas import tpu as pltpu

f32 = jnp.float32
bf16 = jnp.bfloat16
HIGHEST = lax.Precision.HIGHEST

D_MODEL = 2048
HEAD_DIM = 128
MOBA_HEADS = 8
NSA_HEADS = 8
NSA_KV_HEADS = 2
NSA_GROUP = 4
MOBA_W = MOBA_HEADS * HEAD_DIM
NSA_W = NSA_HEADS * HEAD_DIM
NSA_KVW = NSA_KV_HEADS * HEAD_DIM
MOBA_BLOCK = 256
MOBA_TOPK = 3
NSA_CMP_LEN = 32
NSA_CMP_STRIDE = 16
NSA_CMP_HIDDEN = 256
NSA_SEL_BLOCK = 64
NSA_SEL_TOPK = 4
NSA_WINDOW = 512
NSA_BRANCHES = 3
PEER_KEYS = 128
PEER_HEADS = 8
PEER_QDIM = 256
PEER_TOPK = 16
N_MOD = 6
RMS_EPS = 1e-6
PAGE_SIZE = 128
SCALE = HEAD_DIM ** -0.5
NEG = -jnp.inf
LANES = 128

IN_TILE = 512
IN_WIDE_TILES = 11
IN_COLS = IN_WIDE_TILES * IN_TILE + LANES
VMEM_LIMIT = 56 * 1024 * 1024


def _cparams(sem):
    return pltpu.CompilerParams(dimension_semantics=sem, vmem_limit_bytes=VMEM_LIMIT)


def _gelu(x):
    z2 = np.float32(2.0 * np.sqrt(2.0 / np.pi)) * (x + 0.044715 * (x * x * x))
    return x / (1.0 + jnp.exp(-z2))


def _dot_nt(a, b, precision=None):
    return lax.dot_general(a, b, (((1,), (1,)), ((), ())), precision=precision, preferred_element_type=f32)


def _dot(a, b, precision=None):
    return jnp.dot(a, b, precision=precision, preferred_element_type=f32)


def _ada_kernel(c_ref, w_ref, b_ref, o_ref):
    c = c_ref[...]
    a = c * jax.nn.sigmoid(c)
    o_ref[...] = _dot(a, w_ref[...], HIGHEST) + b_ref[...]


def ada_mods(c, w_ada, b_ada):
    rows = c.shape[0]
    n = w_ada.shape[1]
    tn = 1024
    return pl.pallas_call(
        _ada_kernel,
        grid=(n // tn,),
        in_specs=[pl.BlockSpec((rows, D_MODEL), lambda j: (0, 0)),
                  pl.BlockSpec((D_MODEL, tn), lambda j: (0, j)),
                  pl.BlockSpec((1, tn), lambda j: (0, j))],
        out_specs=pl.BlockSpec((rows, tn), lambda j: (0, j)),
        out_shape=jax.ShapeDtypeStruct((rows, n), f32),
        compiler_params=_cparams(("arbitrary",)),
        name="ada_mods",
    )(c, w_ada, b_ada.reshape(1, n))


def _rms_mod(x, g, shift, scale):
    y = x * lax.rsqrt(jnp.mean(x * x, axis=-1, keepdims=True) + RMS_EPS)
    return (y * g) * (1.0 + scale) + shift


def _inproj_kernel(x_ref, g_ref, sh_ref, sc_ref, w_ref, wg_ref,
                   qm_ref, kvm_ref, qn_ref, kvc_ref, kvs_ref, kvw_ref, gt_ref,
                   kvm_out_ref, kvc_out_ref, kvs_out_ref, kvw_out_ref, h_scr):
    j = pl.program_id(1)
    tm = x_ref.shape[0]
    heads_per_tile = IN_TILE // HEAD_DIM

    @pl.when(j == 0)
    def _():
        h = _rms_mod(x_ref[...], g_ref[...], sh_ref[0], sc_ref[0]).astype(bf16)
        h_scr[...] = h
        gt_ref[...] = jax.nn.sigmoid(_dot(h, wg_ref[...]))

    acc = _dot(h_scr[...], w_ref[...])

    @pl.when(j < 2)
    def _():
        qm_ref[...] = acc

    for p in range(2 * MOBA_W // IN_TILE):
        @pl.when(j == 2 + p)
        def _(p=p):
            kvm_ref[...] = acc
            kvm_out_ref[:, heads_per_tile * p:heads_per_tile * (p + 1), :] = acc.reshape(tm, heads_per_tile, HEAD_DIM)

    @pl.when((j >= 6) & (j < 8))
    def _():
        qn_ref[...] = acc

    for p, (flat_ref, out_ref) in enumerate(((kvc_ref, kvc_out_ref), (kvs_ref, kvs_out_ref), (kvw_ref, kvw_out_ref))):
        @pl.when(j == 8 + p)
        def _(flat_ref=flat_ref, out_ref=out_ref):
            flat_ref[...] = acc
            out_ref[...] = acc.reshape(tm, 2, NSA_KV_HEADS, HEAD_DIM)


def in_projection(x, mods3, g_norm1, w_in_p, tm, rows_per_mod):
    t = x.shape[0]
    r = mods3.shape[1]
    tiles_per_mod = rows_per_mod // tm
    mod_spec = lambda which: pl.BlockSpec((1, r, D_MODEL), lambda i, j: (i // tiles_per_mod, 0, which))
    clip = lambda j, lo, n: jnp.clip(j - lo, 0, n - 1)
    out_shapes = [jax.ShapeDtypeStruct((t, w), f32) for w in (MOBA_W, 2 * MOBA_W, NSA_W, 2 * NSA_KVW, 2 * NSA_KVW, 2 * NSA_KVW, LANES)]
    out_specs = [
        pl.BlockSpec((tm, IN_TILE), lambda i, j: (i, clip(j, 0, 2))),
        pl.BlockSpec((tm, IN_TILE), lambda i, j: (i, clip(j, 2, 4))),
        pl.BlockSpec((tm, IN_TILE), lambda i, j: (i, clip(j, 6, 2))),
        pl.BlockSpec((tm, IN_TILE), lambda i, j: (i, 0)),
        pl.BlockSpec((tm, IN_TILE), lambda i, j: (i, 0)),
        pl.BlockSpec((tm, IN_TILE), lambda i, j: (i, 0)),
        pl.BlockSpec((tm, LANES), lambda i, j: (i, 0)),
        pl.BlockSpec((tm, 2 * MOBA_HEADS, HEAD_DIM), lambda i, j: (i, 0, 0)),
    ] + [pl.BlockSpec((tm, 2, NSA_KV_HEADS, HEAD_DIM), lambda i, j: (i, 0, 0, 0))] * 3
    out_shapes += [jax.ShapeDtypeStruct((t, 2 * MOBA_HEADS, HEAD_DIM), f32)]
    out_shapes += [jax.ShapeDtypeStruct((t, 2, NSA_KV_HEADS, HEAD_DIM), f32)] * 3
    return pl.pallas_call(
        _inproj_kernel,
        grid=(t // tm, IN_WIDE_TILES),
        in_specs=[pl.BlockSpec((tm, D_MODEL), lambda i, j: (i, 0)),
                  pl.BlockSpec((1, D_MODEL), lambda i, j: (0, 0)),
                  mod_spec(0), mod_spec(1),
                  pl.BlockSpec((D_MODEL, IN_TILE), lambda i, j: (0, j)),
                  pl.BlockSpec((D_MODEL, LANES), lambda i, j: (0, IN_WIDE_TILES * IN_TILE // LANES))],
        out_specs=out_specs,
        out_shape=out_shapes,
        scratch_shapes=[pltpu.VMEM((tm, D_MODEL), bf16)],
        compiler_params=_cparams(("arbitrary", "arbitrary")),
        name="in_projection",
    )(x, g_norm1.reshape(1, D_MODEL), mods3, mods3, w_in_p, w_in_p)


def _topk_rows(score, k):
    rows = score.shape[0]
    sub = lax.broadcasted_iota(jnp.int32, score.shape, 0)
    sel = jnp.zeros(score.shape, f32)
    g = score
    for _ in range(k):
        m = jnp.max(g, axis=0, keepdims=True)
        hit = (g == m) & (m > NEG)
        idx = jnp.min(jnp.where(hit, sub, rows), axis=0, keepdims=True)
        pick = sub == idx
        sel = jnp.where(pick, 1.0, sel)
        g = jnp.where(pick, NEG, g)
    return sel


def _softmax_first(s, vt):
    m = jnp.max(s, axis=0, keepdims=True)
    p = jnp.exp(s - m)
    return m, jnp.sum(p, axis=0, keepdims=True), _dot(vt, p.astype(bf16))


def _softmax_next(s, vt, m_i, l_i, acc):
    m_new = jnp.maximum(m_i, jnp.max(s, axis=0, keepdims=True))
    alpha = jnp.exp(m_i - m_new)
    p = jnp.exp(s - m_new)
    return m_new, alpha * l_i + jnp.sum(p, axis=0, keepdims=True), alpha * acc + _dot(vt, p.astype(bf16))


MOBA_HEADS_PER_STEP = 8


def _moba_prompt_kernel(slope_ref, q_ref, k_ref, vt_ref, o_ref, sel_ref):
    i = pl.program_id(2)
    mb, dh, hg = MOBA_BLOCK, HEAD_DIM, MOBA_HEADS_PER_STEP
    nb = k_ref.shape[0] // mb
    krow = lax.broadcasted_iota(jnp.int32, (mb, mb), 0)
    qcol = lax.broadcasted_iota(jnp.int32, (mb, mb), 1)
    rel = (qcol - krow).astype(f32)
    blk = lax.broadcasted_iota(jnp.int32, (nb, mb), 0)
    own = pl.multiple_of(i * mb, mb)
    heads = []
    for hh in range(hg):
        cs = slice(hh * dh, (hh + 1) * dh)
        q = q_ref[:, cs]
        slope = slope_ref[0, hh:hh + 1, :1]
        kmean = jnp.concatenate(
            [jnp.mean(k_ref[pl.ds(j * mb, mb), cs], axis=0, keepdims=True) for j in range(nb)], axis=0)
        gate = _dot_nt(kmean, q, HIGHEST)
        sel_ref[hh] = _topk_rows(jnp.where(blk < i, gate, NEG), MOBA_TOPK)
        qb = (q * SCALE).astype(bf16)
        bias = slope * rel
        s = _dot_nt(k_ref[pl.ds(own, mb), cs].astype(bf16), qb) - bias
        s = jnp.where(krow <= qcol, s, NEG)
        heads.append((cs, qb, bias, slope, _softmax_first(s, vt_ref[cs, pl.ds(own, mb)].astype(bf16))))

    def body(j, carry):
        off = pl.multiple_of(j * mb, mb)
        out = []
        for hh, (cs, qb, bias, slope, _) in enumerate(heads):
            s = _dot_nt(k_ref[pl.ds(off, mb), cs].astype(bf16), qb) - bias - slope * ((i - j) * mb).astype(f32)
            s = jnp.where(sel_ref[hh, pl.ds(j, 1), :] > 0.5, s, NEG)
            out.append(_softmax_next(s, vt_ref[cs, pl.ds(off, mb)].astype(bf16), *carry[hh]))
        return tuple(out)

    final = lax.fori_loop(0, i, body, tuple(h[4] for h in heads))
    for hh, (cs, *_rest) in enumerate(heads):
        _, l_i, acc = final[hh]
        o_ref[:, cs] = (acc / l_i).T


def moba_prompt(qm, kvm, vmt, batch, seq):
    nq = seq // MOBA_BLOCK
    hg = MOBA_HEADS_PER_STEP
    wide = hg * HEAD_DIM
    return pl.pallas_call(
        _moba_prompt_kernel,
        grid=(batch, MOBA_HEADS // hg, nq),
        in_specs=[pl.BlockSpec((1, hg, LANES), lambda b, h, i: (h, 0, 0)),
                  pl.BlockSpec((MOBA_BLOCK, wide), lambda b, h, i: (b * nq + i, h)),
                  pl.BlockSpec((seq, wide), lambda b, h, i: (b, h)),
                  pl.BlockSpec((wide, seq), lambda b, h, i: (h, b))],
        out_specs=pl.BlockSpec((MOBA_BLOCK, wide), lambda b, h, i: (b * nq + i, h)),
        out_shape=jax.ShapeDtypeStruct((batch * seq, MOBA_W), f32),
        scratch_shapes=[pltpu.VMEM((hg, seq // MOBA_BLOCK, MOBA_BLOCK), f32)],
        compiler_params=_cparams(("arbitrary", "arbitrary", "arbitrary")),
        name="moba_prompt",
    )(_alibi_groups(MOBA_HEADS // hg, hg), qm, kvm, vmt)


CMP_HALF = NSA_CMP_LEN // 2
N_CMP_ROWS = 128


def _compress_rows(xa, xb, w1_ref, w2_ref):
    half = CMP_HALF * HEAD_DIM
    y = _dot(xa, w1_ref[0, :half, :].astype(bf16))
    z = _dot(xb, w1_ref[0, half:, :].astype(bf16))
    parts = []
    for r in range(y.shape[0] // N_CMP_ROWS):
        zr = z[r * N_CMP_ROWS:(r + 1) * N_CMP_ROWS]
        parts.append(y[r * N_CMP_ROWS:(r + 1) * N_CMP_ROWS] + pltpu.roll(zr, N_CMP_ROWS - 1, 0))
    hid = _gelu(jnp.concatenate(parts, axis=0))
    return _dot(hid.astype(bf16), w2_ref[0].astype(bf16))


def _cmp_prompt_kernel(x0_ref, x1_ref, pe_ref, w1_ref, w2_ref, o_ref):
    pe = pe_ref[0]
    xa, xb = [], []
    for x_ref in (x0_ref, x1_ref):
        pa, pb = [], []
        for l in range(CMP_HALF):
            xl = x_ref[pl.ds(l, N_CMP_ROWS, stride=CMP_HALF), :]
            pa.append((xl + pe[l:l + 1]).astype(bf16))
            pb.append((xl + pe[CMP_HALF + l:CMP_HALF + l + 1]).astype(bf16))
        xa.append(jnp.concatenate(pa, axis=1))
        xb.append(jnp.concatenate(pb, axis=1))
    out = _compress_rows(jnp.concatenate(xa, axis=0), jnp.concatenate(xb, axis=0), w1_ref, w2_ref)
    for g in range(NSA_KV_HEADS):
        o_ref[0, 0, g] = out[g * N_CMP_ROWS:(g + 1) * N_CMP_ROWS]


def compress_prompt(kvc, pe2, w12, w22, batch, seq):
    return pl.pallas_call(
        _cmp_prompt_kernel,
        grid=(batch, 2),
        in_specs=[pl.BlockSpec((seq, HEAD_DIM), lambda b, kv: (b, NSA_KV_HEADS * kv)),
                  pl.BlockSpec((seq, HEAD_DIM), lambda b, kv: (b, NSA_KV_HEADS * kv + 1)),
                  pl.BlockSpec((1, NSA_CMP_LEN, HEAD_DIM), lambda b, kv: (kv, 0, 0)),
                  pl.BlockSpec((1, NSA_CMP_LEN * HEAD_DIM, NSA_CMP_HIDDEN), lambda b, kv: (kv, 0, 0)),
                  pl.BlockSpec((1, NSA_CMP_HIDDEN, HEAD_DIM), lambda b, kv: (kv, 0, 0))],
        out_specs=pl.BlockSpec((1, 1, NSA_KV_HEADS, N_CMP_ROWS, HEAD_DIM), lambda b, kv: (b, kv, 0, 0, 0)),
        out_shape=jax.ShapeDtypeStruct((batch, 2, NSA_KV_HEADS, N_CMP_ROWS, HEAD_DIM), f32),
        compiler_params=_cparams(("arbitrary", "arbitrary")),
        name="compress_prompt",
    )(kvc, kvc, pe2, w12, w22)


NSA_TQ = 128
N_CMP = 127


def _masked_softmax(s, mask):
    s = jnp.where(mask, s, NEG)
    m = jnp.max(s, axis=-1, keepdims=True)
    m = jnp.where(m > NEG, m, 0.0)
    p = jnp.where(mask, jnp.exp(s - m), 0.0)
    return p / jnp.maximum(jnp.sum(p, axis=-1, keepdims=True), 1e-30)


def _overlap_matrix():
    c = lax.broadcasted_iota(jnp.int32, (LANES, LANES), 0)
    j = lax.broadcasted_iota(jnp.int32, (LANES, LANES), 1)
    cs = NSA_CMP_STRIDE * c
    bs = NSA_SEL_BLOCK * j
    return ((cs < bs + NSA_SEL_BLOCK) & (cs + NSA_CMP_LEN - 1 >= bs)).astype(f32)


NSA_TK = 256
N_SEL_BLOCKS = 32


def _nsa_prompt_kernel(slope_ref, q_ref, gt_ref, ck_ref, cvt_ref, ks_ref, vst_ref, kw_ref, vwt_ref, o_ref, sel_ref):
    i = pl.program_id(1)
    tq, tk, R, G, ls, dh = NSA_TQ, NSA_TK, NSA_GROUP, NSA_KV_HEADS, NSA_SEL_BLOCK, HEAD_DIM
    W = R * tq
    q_all = q_ref[...]
    gates_t = gt_ref[...].T
    n_lane = lax.broadcasted_iota(jnp.int32, (1, W), 1) & (tq - 1)
    t_lane = i * tq + n_lane
    t_q = i * tq + lax.broadcasted_iota(jnp.int32, (N_SEL_BLOCKS, tq), 1)
    jrow = lax.broadcasted_iota(jnp.int32, (N_SEL_BLOCKS, tq), 0)
    crow = lax.broadcasted_iota(jnp.int32, (N_CMP_ROWS, W), 0)
    visible = (NSA_CMP_STRIDE * crow + NSA_CMP_LEN - 1 <= t_lane) & (crow < N_CMP)
    oj = lax.broadcasted_iota(jnp.int32, (N_SEL_BLOCKS, N_CMP_ROWS), 0) * ls
    oc = lax.broadcasted_iota(jnp.int32, (N_SEL_BLOCKS, N_CMP_ROWS), 1) * NSA_CMP_STRIDE
    overlap_t = ((oc < oj + ls) & (oc + NSA_CMP_LEN - 1 >= oj)).astype(f32)
    krow = lax.broadcasted_iota(jnp.int32, (tk, W), 0)
    rel_i = n_lane - krow
    rel = rel_i.astype(f32)
    diag = (i * tq) // tk
    diag_off = pl.multiple_of(diag * tk, tk)

    def chosen_rows(g, kt):
        parts = [jnp.broadcast_to(sel_ref[g, pl.ds(kt * (tk // ls) + b, 1), :], (ls, W)) for b in range(tk // ls)]
        return jnp.concatenate(parts, axis=0) > 0.5

    groups = []
    for g in range(G):
        gs = slice(g * dh, (g + 1) * dh)
        qs = jnp.concatenate([q_all[:, (g * R + r) * dh:(g * R + r + 1) * dh] for r in range(R)], axis=0)
        slope = slope_ref[g]
        s_c = jnp.where(visible, _dot_nt(ck_ref[0, g], qs, HIGHEST) * SCALE, NEG)
        m_c = jnp.max(s_c, axis=0, keepdims=True)
        p_c = jnp.where(visible, jnp.exp(s_c - jnp.where(m_c > NEG, m_c, 0.0)), 0.0)
        p_c = p_c / jnp.maximum(jnp.sum(p_c, axis=0, keepdims=True), 1e-30)
        o_c = _dot(cvt_ref[0, g].astype(bf16), p_c.astype(bf16))
        p_sum = p_c[:, 0:tq]
        for r in range(1, R):
            p_sum = p_sum + p_c[:, r * tq:(r + 1) * tq]
        imp = _dot(overlap_t, p_sum, HIGHEST)
        sel = _topk_rows(jnp.where(jrow < t_q // ls, imp, NEG), NSA_SEL_TOPK)
        sel_ref[g] = jnp.concatenate([sel] * R, axis=1)
        qb = (qs * SCALE).astype(bf16)
        bias = slope * rel
        d0 = slope * (i * tq - diag * tk).astype(f32)
        dist = rel_i + (i * tq - diag * tk)
        key_blk = (diag * tk + krow) // ls
        s = _dot_nt(ks_ref[pl.ds(diag_off, tk), gs].astype(bf16), qb) - bias - d0
        ok = chosen_rows(g, diag) | ((key_blk == t_lane // ls) & (dist >= 0))
        slc0 = _softmax_first(jnp.where(ok, s, NEG), vst_ref[gs, pl.ds(diag_off, tk)].astype(bf16))
        s = _dot_nt(kw_ref[pl.ds(diag_off, tk), gs].astype(bf16), qb) - bias - d0
        win0 = _softmax_first(jnp.where((dist >= 0) & (dist < NSA_WINDOW), s, NEG),
                              vwt_ref[gs, pl.ds(diag_off, tk)].astype(bf16))
        groups.append((gs, qb, bias, slope, o_c, slc0, win0))

    def slc_body(kt, carry):
        off = pl.multiple_of(kt * tk, tk)
        out = []
        for g, (gs, qb, bias, slope, *_rest) in enumerate(groups):
            s = _dot_nt(ks_ref[pl.ds(off, tk), gs].astype(bf16), qb) - bias - slope * (i * tq - kt * tk).astype(f32)
            s = jnp.where(chosen_rows(g, kt), s, NEG)
            out.append(_softmax_next(s, vst_ref[gs, pl.ds(off, tk)].astype(bf16), *carry[g]))
        return tuple(out)

    def win_body(kt, carry):
        off = pl.multiple_of(kt * tk, tk)
        shift = i * tq - kt * tk
        out = []
        for g, (gs, qb, bias, slope, *_rest) in enumerate(groups):
            s = _dot_nt(kw_ref[pl.ds(off, tk), gs].astype(bf16), qb) - bias - slope * shift.astype(f32)
            s = jnp.where(rel_i + shift < NSA_WINDOW, s, NEG)
            out.append(_softmax_next(s, vwt_ref[gs, pl.ds(off, tk)].astype(bf16), *carry[g]))
        return tuple(out)

    slc = lax.fori_loop(0, diag, slc_body, tuple(grp[5] for grp in groups))
    first_win = jnp.maximum(i * tq - (NSA_WINDOW - 1), 0) // tk
    win = lax.fori_loop(first_win, diag, win_body, tuple(grp[6] for grp in groups))

    for g, (gs, qb, bias, slope, o_c, *_rest) in enumerate(groups):
        def gate_row(branch):
            return jnp.concatenate([gates_t[NSA_BRANCHES * (g * R + r) + branch:NSA_BRANCHES * (g * R + r) + branch + 1]
                                    for r in range(R)], axis=1)
        o_t = (gate_row(0) * o_c + gate_row(1) * (slc[g][2] / slc[g][1]) + gate_row(2) * (win[g][2] / win[g][1]))
        o = o_t.T
        for r in range(R):
            o_ref[:, (g * R + r) * dh:(g * R + r + 1) * dh] = o[r * tq:(r + 1) * tq]


def _alibi_groups(n_groups, group):
    n = n_groups * group
    s = (2.0 ** (-8.0 * np.arange(1, n + 1) / n)).reshape(n_groups, group)
    return jnp.asarray(np.broadcast_to(s[:, :, None], (n_groups, group, LANES)), dtype=f32)


def nsa_prompt(qn, gates, ck, cvt, kvs, vst, kvw, vwt, batch, seq):
    assert seq // NSA_SEL_BLOCK == N_SEL_BLOCKS and (seq - NSA_CMP_LEN) // NSA_CMP_STRIDE + 1 == N_CMP
    nq = seq // NSA_TQ
    G, R = NSA_KV_HEADS, NSA_GROUP
    n = G * R
    slopes = (2.0 ** (-8.0 * np.arange(1, n + 1) / n)).reshape(G, 1, R, 1)
    slope_lanes = jnp.asarray(np.broadcast_to(slopes, (G, 1, R, NSA_TQ)).reshape(G, 1, R * NSA_TQ), dtype=f32)
    full = lambda a: pl.BlockSpec(a.shape, lambda b, i: (0,) * a.ndim)
    return pl.pallas_call(
        _nsa_prompt_kernel,
        grid=(batch, nq),
        in_specs=[full(slope_lanes),
                  pl.BlockSpec((NSA_TQ, NSA_W), lambda b, i: (b * nq + i, 0)),
                  pl.BlockSpec((NSA_TQ, LANES), lambda b, i: (b * nq + i, 0)),
                  pl.BlockSpec((1, G, N_CMP_ROWS, HEAD_DIM), lambda b, i: (b, 0, 0, 0)),
                  pl.BlockSpec((1, G, HEAD_DIM, N_CMP_ROWS), lambda b, i: (b, 0, 0, 0)),
                  pl.BlockSpec((seq, NSA_KVW), lambda b, i: (b, 0)),
                  pl.BlockSpec((NSA_KVW, seq), lambda b, i: (0, b)),
                  pl.BlockSpec((seq, NSA_KVW), lambda b, i: (b, 0)),
                  pl.BlockSpec((NSA_KVW, seq), lambda b, i: (0, b))],
        out_specs=pl.BlockSpec((NSA_TQ, NSA_W), lambda b, i: (b * nq + i, 0)),
        out_shape=jax.ShapeDtypeStruct((batch * seq, NSA_W), f32),
        scratch_shapes=[pltpu.VMEM((G, N_SEL_BLOCKS, R * NSA_TQ), f32)],
        compiler_params=_cparams(("arbitrary", "arbitrary")),
        name="nsa_prompt",
    )(slope_lanes, qn, gates, ck, cvt, kvs, vst, kvw, vwt)


def _head_slopes(n):
    s = 2.0 ** (-8.0 * np.arange(1, n + 1) / n)
    return jnp.asarray(np.broadcast_to(s[:, None], (n, LANES)), dtype=f32)


def _moba_sample_kernel(pt_ref, slope_ref, q_ref, kvn_ref, *refs):
    page_refs, o_ref = refs[:-1], refs[-1]
    n_pages = len(page_refs)
    mb = MOBA_BLOCK
    ppb = mb // PAGE_SIZE
    nb = n_pages // ppb
    t_new = n_pages * PAGE_SIZE
    q = q_ref[0]
    slope = slope_ref[...]
    tok = lax.broadcasted_iota(jnp.int32, (mb, MOBA_HEADS, LANES), 0)
    ones = jnp.ones((HEAD_DIM, LANES), bf16)
    gates, ms, ls, os_ = [], [], [], []
    for j in range(nb):
        k = jnp.concatenate([page_refs[ppb * j + u][pl.ds(0, PAGE_SIZE, stride=2)] for u in range(ppb)], axis=0)
        v = jnp.concatenate([page_refs[ppb * j + u][pl.ds(1, PAGE_SIZE, stride=2)] for u in range(ppb)], axis=0)
        kmean = jnp.sum(k, axis=0) / mb
        gates.append(jnp.sum(q * kmean, axis=-1, keepdims=True))
        dist = (t_new - j * mb - tok).astype(f32)
        kq = (k * q[None]).reshape(mb * MOBA_HEADS, HEAD_DIM).astype(bf16)
        s = _dot(kq, ones).reshape(mb, MOBA_HEADS, LANES) * SCALE - slope[None] * dist
        m = jnp.max(s, axis=0)
        p = jnp.exp(s - m[None])
        ms.append(m)
        ls.append(jnp.sum(p, axis=0))
        os_.append(jnp.sum(p * v, axis=0))
    chosen = []
    for j in range(nb):
        rank = jnp.zeros_like(gates[j])
        for j2 in range(nb):
            if j2 != j:
                ahead = (gates[j2] >= gates[j]) if j2 < j else (gates[j2] > gates[j])
                rank = rank + ahead.astype(f32)
        chosen.append(rank < MOBA_TOPK)
    kn, vn = kvn_ref[0, 0], kvn_ref[0, 1]
    s_own = jnp.sum(q * kn, axis=-1, keepdims=True) * SCALE
    m_all = s_own
    for j in range(nb):
        m_all = jnp.maximum(m_all, jnp.where(chosen[j], ms[j], NEG))
    w_own = jnp.exp(s_own - m_all)
    l_all = w_own
    o_all = w_own * vn
    for j in range(nb):
        w = jnp.where(chosen[j], jnp.exp(ms[j] - m_all), 0.0)
        l_all = l_all + w * ls[j]
        o_all = o_all + w * os_[j]
    o_ref[0] = o_all / l_all


def moba_sample(q3, kvn4, cache3, page_table):
    nseq, n_pages = page_table.shape
    rows = 2 * PAGE_SIZE
    page_spec = lambda p: pl.BlockSpec((rows, MOBA_HEADS, HEAD_DIM), lambda b, pt: (pt[b, p], 0, 0))
    return pl.pallas_call(
        _moba_sample_kernel,
        grid_spec=pltpu.PrefetchScalarGridSpec(
            num_scalar_prefetch=1, grid=(nseq,),
            in_specs=[pl.BlockSpec((MOBA_HEADS, LANES), lambda b, pt: (0, 0)),
                      pl.BlockSpec((1, MOBA_HEADS, HEAD_DIM), lambda b, pt: (b, 0, 0)),
                      pl.BlockSpec((1, 2, MOBA_HEADS, HEAD_DIM), lambda b, pt: (b, 0, 0, 0))]
                     + [page_spec(p) for p in range(n_pages)],
            out_specs=pl.BlockSpec((1, MOBA_HEADS, HEAD_DIM), lambda b, pt: (b, 0, 0))),
        out_shape=jax.ShapeDtypeStruct((nseq, MOBA_HEADS, HEAD_DIM), f32),
        compiler_params=_cparams(("arbitrary",)),
        name="moba_sample",
    )(page_table, _head_slopes(MOBA_HEADS), q3, kvn4, *([cache3] * n_pages))


KV_ROWS = 2 * NSA_KV_HEADS


COMPRESS_BATCH = 2


def _nsa_sample_compress_kernel(pt_ref, pe_ref, w1k_ref, w1v_ref, w2k_ref, w2v_ref, *refs):
    page_refs, ckv_ref = refs[:-1], refs[-1]
    n_pages = len(page_refs) // COMPRESS_BATCH
    per_page = PAGE_SIZE // CMP_HALF
    G = NSA_KV_HEADS
    for kv, (w1_ref, w2_ref) in enumerate(((w1k_ref, w2k_ref), (w1v_ref, w2v_ref))):
        pe = pe_ref[kv]
        xa, xb = [], []
        for s in range(COMPRESS_BATCH):
            pages = page_refs[s * n_pages:(s + 1) * n_pages]
            for g in range(G):
                pa, pb = [], []
                for l in range(CMP_HALF):
                    xl = jnp.concatenate(
                        [pr[pl.ds(KV_ROWS * l + G * kv + g, per_page, stride=KV_ROWS * CMP_HALF), :] for pr in pages], axis=0)
                    pa.append((xl + pe[l:l + 1]).astype(bf16))
                    pb.append((xl + pe[CMP_HALF + l:CMP_HALF + l + 1]).astype(bf16))
                xa.append(jnp.concatenate(pa, axis=1))
                xb.append(jnp.concatenate(pb, axis=1))
        comp = _compress_rows(jnp.concatenate(xa, axis=0), jnp.concatenate(xb, axis=0), w1_ref, w2_ref)
        for s in range(COMPRESS_BATCH):
            for g in range(G):
                ckv_ref[s, kv, g] = comp[(s * G + g) * N_CMP_ROWS:(s * G + g + 1) * N_CMP_ROWS]


SELECT_BATCH = 8


def _nsa_sample_select_kernel(q_ref, ckv_ref, oc_ref, sel_ref, *, t_new):
    sb = q_ref.shape[0]
    G, R, H = NSA_KV_HEADS, NSA_GROUP, NSA_HEADS
    row = lax.broadcasted_iota(jnp.int32, (H, LANES), 0)
    lane = lax.broadcasted_iota(jnp.int32, (sb * H, LANES), 1)
    n_cmp = (t_new + 1 - NSA_CMP_LEN) // NSA_CMP_STRIDE + 1
    visible = (lane < n_cmp) & (NSA_CMP_STRIDE * lane + NSA_CMP_LEN - 1 <= t_new)
    first = row < R
    s_c = jnp.concatenate(
        [jnp.where(first, _dot_nt(q_ref[b], ckv_ref[b, 0, 0], HIGHEST), _dot_nt(q_ref[b], ckv_ref[b, 0, 1], HIGHEST))
         for b in range(sb)], axis=0) * SCALE
    p_c = _masked_softmax(s_c, visible)
    p_b = p_c.astype(bf16)
    for b in range(sb):
        pb = p_b[b * H:(b + 1) * H]
        oc_ref[b] = jnp.where(first, _dot(pb, ckv_ref[b, 1, 0].astype(bf16)), _dot(pb, ckv_ref[b, 1, 1].astype(bf16)))
    gi = lax.broadcasted_iota(jnp.int32, (sb * G, sb * H), 0)
    gj = lax.broadcasted_iota(jnp.int32, (sb * G, sb * H), 1)
    p_sum = _dot((gj // R == gi).astype(f32), p_c, HIGHEST)
    imp = _dot(p_sum, _overlap_matrix(), HIGHEST)
    lane_g = lax.broadcasted_iota(jnp.int32, (sb * G, LANES), 1)
    cur = jnp.where(lane_g < t_new // NSA_SEL_BLOCK, imp, NEG)
    out = jnp.zeros((sb * G, LANES), jnp.int32)
    for s in range(NSA_SEL_TOPK):
        m = jnp.max(cur, axis=-1, keepdims=True)
        idx = jnp.min(jnp.where((cur == m) & (m > NEG), lane_g, LANES), axis=-1, keepdims=True)
        found = idx < LANES
        out = jnp.where(lane_g == s, jnp.where(found, idx, 0), out)
        out = jnp.where(lane_g == NSA_SEL_TOPK + s, found.astype(jnp.int32), out)
        cur = jnp.where(lane_g == idx, NEG, cur)
    sel_ref[...] = out


def nsa_sample_cmp(q3, cache2, page_table, pe2, w1k, w1v, w2k, w2v):
    nseq, n_pages = page_table.shape
    G = NSA_KV_HEADS
    rows = PAGE_SIZE * KV_ROWS
    cb = COMPRESS_BATCH
    page_spec = lambda s, p: pl.BlockSpec((rows, HEAD_DIM), lambda b, pt: (pt[cb * b + s, p], 0))
    full = lambda a: pl.BlockSpec(a.shape, lambda b, pt: (0,) * a.ndim)
    ckv = pl.pallas_call(
        _nsa_sample_compress_kernel,
        grid_spec=pltpu.PrefetchScalarGridSpec(
            num_scalar_prefetch=1, grid=(nseq // cb,),
            in_specs=[full(pe2), full(w1k), full(w1v), full(w2k), full(w2v)]
                     + [page_spec(s, p) for s in range(cb) for p in range(n_pages)],
            out_specs=pl.BlockSpec((cb, 2, G, N_CMP_ROWS, HEAD_DIM), lambda b, pt: (b, 0, 0, 0, 0))),
        out_shape=jax.ShapeDtypeStruct((nseq, 2, G, N_CMP_ROWS, HEAD_DIM), f32),
        compiler_params=_cparams(("arbitrary",)),
        name="nsa_sample_compress",
    )(page_table, pe2, w1k, w1v, w2k, w2v, *([cache2] * (cb * n_pages)))
    sb = SELECT_BATCH
    return pl.pallas_call(
        functools.partial(_nsa_sample_select_kernel, t_new=n_pages * PAGE_SIZE),
        grid=(nseq // sb,),
        in_specs=[pl.BlockSpec((sb, NSA_HEADS, HEAD_DIM), lambda i: (i, 0, 0)),
                  pl.BlockSpec((sb, 2, G, N_CMP_ROWS, HEAD_DIM), lambda i: (i, 0, 0, 0, 0))],
        out_specs=[pl.BlockSpec((sb, NSA_HEADS, HEAD_DIM), lambda i: (i, 0, 0)),
                   pl.BlockSpec((sb * G, LANES), lambda i: (i, 0))],
        out_shape=[jax.ShapeDtypeStruct((nseq, NSA_HEADS, HEAD_DIM), f32),
                   jax.ShapeDtypeStruct((nseq * G, LANES), jnp.int32)],
        compiler_params=_cparams(("arbitrary",)),
        name="nsa_sample_select",
    )(q3, ckv)


def _decode_attend(q, slope, keys, vals, pos, valid, k_own, v_own, t_new):
    s = _dot_nt(q.astype(bf16), keys.astype(bf16)) * SCALE - slope * (t_new - pos).astype(f32)
    s = jnp.where(valid, s, NEG)
    s_own = jnp.sum(q * k_own, axis=-1, keepdims=True) * SCALE
    m = jnp.maximum(jnp.max(s, axis=-1, keepdims=True), s_own)
    p = jnp.exp(s - m)
    p_own = jnp.exp(s_own - m)
    denom = jnp.sum(p, axis=-1, keepdims=True) + p_own
    return (_dot(p.astype(bf16), vals.astype(bf16)) + p_own * v_own) / denom


def _nsa_sample_attn_kernel(pt_ref, sel_ref, slope_ref, q_ref, gt_ref, oc_ref, ksn_ref, kwn_ref, win_ref, *refs):
    blk_refs, (o_ref, wout_ref) = refs[:-2], refs[-2:]
    b = pl.program_id(0)
    G, R, ls, K = NSA_KV_HEADS, NSA_GROUP, NSA_SEL_BLOCK, NSA_SEL_TOPK
    t_new = pt_ref.shape[1] * PAGE_SIZE
    q = q_ref[0]
    slope = slope_ref[:, :1]
    row = lax.broadcasted_iota(jnp.int32, (G * R, HEAD_DIM), 0)
    own_rows = lambda ref, kv: jnp.where(row < R, ref[0, G * kv:G * kv + 1], ref[0, G * kv + 1:G * kv + 2])
    lane_s = lax.broadcasted_iota(jnp.int32, (1, K * ls), 1)
    n_win = win_ref.shape[0] // KV_ROWS
    lane_w = lax.broadcasted_iota(jnp.int32, (1, n_win), 1)
    pos_w = t_new - n_win + lane_w
    o_s = jnp.zeros((G * R, HEAD_DIM), f32)
    o_w = jnp.zeros((G * R, HEAD_DIM), f32)
    for g in range(G):
        keys = jnp.concatenate([blk_refs[g * K + s][pl.ds(g, ls, stride=KV_ROWS), :] for s in range(K)], axis=0)
        vals = jnp.concatenate([blk_refs[g * K + s][pl.ds(G + g, ls, stride=KV_ROWS), :] for s in range(K)], axis=0)
        pos = jnp.zeros((1, K * ls), jnp.int32)
        valid = jnp.zeros((1, K * ls), jnp.int32)
        for s in range(K):
            here = lane_s // ls == s
            pos = jnp.where(here, sel_ref[b, g * 2 * K + s] * ls + lane_s - s * ls, pos)
            valid = jnp.where(here, sel_ref[b, g * 2 * K + K + s], valid)
        mine = (row >= g * R) & (row < (g + 1) * R)
        o_s = jnp.where(mine, _decode_attend(q, slope, keys, vals, pos, valid > 0, own_rows(ksn_ref, 0), own_rows(ksn_ref, 1), t_new), o_s)
        keys_w = win_ref[pl.ds(g, n_win, stride=KV_ROWS), :]
        vals_w = win_ref[pl.ds(G + g, n_win, stride=KV_ROWS), :]
        ok_w = (pos_w > t_new - NSA_WINDOW) & (pos_w >= 0)
        o_w = jnp.where(mine, _decode_attend(q, slope, keys_w, vals_w, pos_w, ok_w, own_rows(kwn_ref, 0), own_rows(kwn_ref, 1), t_new), o_w)
    gt = gt_ref[0]
    o_ref[0] = gt[:, 0:1] * oc_ref[0] + gt[:, 1:2] * o_s + gt[:, 2:3] * o_w
    total = win_ref.shape[0]
    shifted = pltpu.roll(win_ref[...], total - KV_ROWS, 0)
    new8 = jnp.concatenate([kwn_ref[0], kwn_ref[0]], axis=0)
    row8 = lax.broadcasted_iota(jnp.int32, (2 * KV_ROWS, HEAD_DIM), 0)
    wout_ref[pl.ds(0, total - 2 * KV_ROWS), :] = shifted[:total - 2 * KV_ROWS]
    wout_ref[pl.ds(total - 2 * KV_ROWS, 2 * KV_ROWS), :] = jnp.where(row8 >= KV_ROWS, new8, shifted[total - 2 * KV_ROWS:])


def nsa_sample_attn(q3, gates3, o_c, ksn, kwn, slc2, win2, page_table, sel_flat):
    nseq = page_table.shape[0]
    G, K, ls = NSA_KV_HEADS, NSA_SEL_TOPK, NSA_SEL_BLOCK
    blocks_per_page = PAGE_SIZE // ls
    n_blocks = page_table.shape[1] * blocks_per_page
    win_rows = win2.shape[0] // nseq

    def blk_spec(g, s):
        def index(b, pt, sel):
            bb = jnp.minimum(b, nseq - 1)
            blk = jnp.clip(sel[bb, g * 2 * K + s], 0, n_blocks - 1)
            return (pt[bb, blk // blocks_per_page] * blocks_per_page + blk % blocks_per_page, 0)
        return pl.BlockSpec((ls * KV_ROWS, HEAD_DIM), index)

    per_seq = lambda shape: pl.BlockSpec((1,) + shape, lambda b, pt, sel: (b,) + (0,) * len(shape))
    return pl.pallas_call(
        _nsa_sample_attn_kernel,
        grid_spec=pltpu.PrefetchScalarGridSpec(
            num_scalar_prefetch=2, grid=(nseq,),
            in_specs=[pl.BlockSpec((NSA_HEADS, LANES), lambda b, pt, sel: (0, 0)),
                      per_seq((NSA_HEADS, HEAD_DIM)), per_seq((NSA_HEADS, LANES)), per_seq((NSA_HEADS, HEAD_DIM)),
                      per_seq((KV_ROWS, HEAD_DIM)), per_seq((KV_ROWS, HEAD_DIM)),
                      pl.BlockSpec((win_rows, HEAD_DIM), lambda b, pt, sel: (b, 0))]
                     + [blk_spec(g, s) for g in range(G) for s in range(K)],
            out_specs=[per_seq((NSA_HEADS, HEAD_DIM)),
                       pl.BlockSpec((win_rows, HEAD_DIM), lambda b, pt, sel: (b, 0))]),
        out_shape=[jax.ShapeDtypeStruct((nseq, NSA_HEADS, HEAD_DIM), f32),
                   jax.ShapeDtypeStruct(win2.shape, f32)],
        compiler_params=_cparams(("arbitrary",)),
        name="nsa_sample_attn",
    )(page_table, sel_flat, _head_slopes(NSA_HEADS), q3, gates3, o_c, ksn, kwn, win2, *([slc2] * (G * K)))


def _mid_kernel(x_ref, om_ref, on_ref, w_ref, g_ref, gate_ref, sh_ref, sc_ref, x1_ref, h2_ref):
    proj = (_dot(om_ref[...].astype(bf16), w_ref[:MOBA_W, :]) + _dot(on_ref[...].astype(bf16), w_ref[MOBA_W:, :]))
    x1 = x_ref[...] + gate_ref[0] * proj
    x1_ref[...] = x1
    h2_ref[...] = _rms_mod(x1, g_ref[...], sh_ref[0], sc_ref[0]).astype(bf16)


def mid_block(x, o_m, o_n, w_out_b, g_norm2, mods3, tm, rows_per_mod):
    t = x.shape[0]
    r = mods3.shape[1]
    tiles_per_mod = rows_per_mod // tm
    mod_spec = lambda which: pl.BlockSpec((1, r, D_MODEL), lambda i: (i // tiles_per_mod, 0, which))
    return pl.pallas_call(
        _mid_kernel,
        grid=(t // tm,),
        in_specs=[pl.BlockSpec((tm, D_MODEL), lambda i: (i, 0)),
                  pl.BlockSpec((tm, MOBA_W), lambda i: (i, 0)),
                  pl.BlockSpec((tm, NSA_W), lambda i: (i, 0)),
                  pl.BlockSpec((MOBA_W + NSA_W, D_MODEL), lambda i: (0, 0)),
                  pl.BlockSpec((1, D_MODEL), lambda i: (0, 0)),
                  mod_spec(2), mod_spec(3), mod_spec(4)],
        out_specs=[pl.BlockSpec((tm, D_MODEL), lambda i: (i, 0)),
                   pl.BlockSpec((tm, D_MODEL), lambda i: (i, 0))],
        out_shape=[jax.ShapeDtypeStruct((t, D_MODEL), f32), jax.ShapeDtypeStruct((t, D_MODEL), bf16)],
        compiler_params=_cparams(("arbitrary",)),
        name="mid_block",
    )(x, o_m, o_n, w_out_b, g_norm2.reshape(1, D_MODEL), mods3, mods3, mods3)


PEER_HALF = PEER_QDIM // 2
PEER_A_FULL = 8


def _top_values(s, k):
    tops = []
    cur = s
    for _ in range(k):
        m = jnp.max(cur, axis=0, keepdims=True)
        tops.append(m)
        cur = jnp.where(cur == m, NEG, cur)
    return jnp.concatenate(tops, axis=0)


def _peer_route_kernel(h_ref, wq_ref, keys_ref, s1_ref, thr_ref, e1_ref, coef_ref):
    q = _dot(h_ref[...], wq_ref[...])
    k = PEER_TOPK
    for h in range(PEER_HEADS):
        base = h * PEER_QDIM
        s0 = _dot_nt(keys_ref[h, 0], q[:, base:base + PEER_HALF], HIGHEST)
        s1 = _dot_nt(keys_ref[h, 1], q[:, base + PEER_HALF:base + PEER_QDIM], HIGHEST)
        top0 = _top_values(s0, k)
        top1 = _top_values(s1, k)
        cand = jnp.concatenate([top0[a:a + 1] + top1 for a in range(PEER_A_FULL)]
                               + [top0[PEER_A_FULL:] + top1[0:1]], axis=0)
        best = _top_values(cand, k)
        tau = best[k - 1:k]
        z = jnp.sum(jnp.exp(best - best[0:1]), axis=0, keepdims=True)
        thr = jnp.full(s0.shape, jnp.inf, f32)
        for a in range(k):
            thr_a = jnp.min(jnp.where(top0[a:a + 1] + top1 >= tau, top1, jnp.inf), axis=0, keepdims=True)
            thr = jnp.where(s0 == top0[a:a + 1], thr_a, thr)
        s1_ref[h] = s1
        thr_ref[h] = thr
        e1_ref[h] = jnp.exp(s1 - top1[0:1])
        coef_ref[h] = jnp.exp(s0 - top0[0:1]) / z


def peer_route(h2, wq_b, keys, tm):
    t = h2.shape[0]
    out_spec = pl.BlockSpec((PEER_HEADS, PEER_KEYS, tm), lambda i: (0, 0, i))
    out_shape = jax.ShapeDtypeStruct((PEER_HEADS, PEER_KEYS, t), f32)
    return pl.pallas_call(
        _peer_route_kernel,
        grid=(t // tm,),
        in_specs=[pl.BlockSpec((tm, D_MODEL), lambda i: (i, 0)),
                  pl.BlockSpec(wq_b.shape, lambda i: (0, 0)),
                  pl.BlockSpec(keys.shape, lambda i: (0, 0, 0, 0))],
        out_specs=[out_spec] * 4,
        out_shape=[out_shape] * 4,
        compiler_params=_cparams(("arbitrary",)),
        name="peer_route",
    )(h2, wq_b, keys)


PEER_EXPERT_TILE = 1024
PEER_SLAB = 512


def _peer_expert_kernel(ht_ref, u_ref, vt_ref, s1_ref, thr_ref, e1_ref, coef_ref, x1_ref, gate_ref, gf_ref,
                        y_ref, acc_ref, p_ref):
    e = pl.program_id(1)
    te = u_ref.shape[0]
    rows_per_step = te // PEER_KEYS

    @pl.when(e == 0)
    def _():
        acc_ref[...] = jnp.zeros_like(acc_ref)

    tm = ht_ref.shape[1]
    for slab in range(te // PEER_SLAB):
        act = _gelu(_dot(u_ref[slab * PEER_SLAB:(slab + 1) * PEER_SLAB, :], ht_ref[...]))
        for rr in range(PEER_SLAB // PEER_KEYS):
            r = slab * (PEER_SLAB // PEER_KEYS) + rr
            i0 = e * rows_per_step + r
            for c in range(tm // LANES):
                cs = slice(c * LANES, (c + 1) * LANES)
                w = jnp.zeros((PEER_KEYS, LANES), f32)
                for h in range(PEER_HEADS):
                    thr = thr_ref[h, pl.ds(i0, 1), :][:, cs]
                    coef = coef_ref[h, pl.ds(i0, 1), :][:, cs]
                    w = w + jnp.where(s1_ref[h, :, cs] >= thr, e1_ref[h, :, cs] * coef, 0.0)
                p_ref[r * PEER_KEYS:(r + 1) * PEER_KEYS, cs] = (
                    w * act[rr * PEER_KEYS:(rr + 1) * PEER_KEYS, cs]).astype(bf16)
    acc_ref[...] += _dot(vt_ref[...], p_ref[...])

    @pl.when(e == pl.num_programs(1) - 1)
    def _():
        y = x1_ref[...] + gate_ref[0] * acc_ref[...].T
        y_ref[...] = (y * lax.rsqrt(jnp.mean(y * y, axis=-1, keepdims=True) + RMS_EPS)) * gf_ref[...]


def peer_experts(h2t, u_b, vt_b, route, x1, mods3, g_final, tm, te, rows_per_mod):
    t = h2t.shape[1]
    r = mods3.shape[1]
    n_exp = u_b.shape[0]
    tiles_per_mod = rows_per_mod // tm
    once = dict(pipeline_mode=pl.Buffered(1))
    route_spec = pl.BlockSpec((PEER_HEADS, PEER_KEYS, tm), lambda i, e: (0, 0, i), **once)
    return pl.pallas_call(
        _peer_expert_kernel,
        grid=(t // tm, n_exp // te),
        in_specs=[pl.BlockSpec((D_MODEL, tm), lambda i, e: (0, i), **once),
                  pl.BlockSpec((te, D_MODEL), lambda i, e: (e, 0)),
                  pl.BlockSpec((D_MODEL, te), lambda i, e: (0, e)),
                  route_spec, route_spec, route_spec, route_spec,
                  pl.BlockSpec((tm, D_MODEL), lambda i, e: (i, 0), **once),
                  pl.BlockSpec((1, r, D_MODEL), lambda i, e: (i // tiles_per_mod, 0, 5)),
                  pl.BlockSpec((1, D_MODEL), lambda i, e: (0, 0))],
        out_specs=pl.BlockSpec((tm, D_MODEL), lambda i, e: (i, 0)),
        out_shape=jax.ShapeDtypeStruct((t, D_MODEL), f32),
        scratch_shapes=[pltpu.VMEM((D_MODEL, tm), f32), pltpu.VMEM((te, tm), bf16)],
        compiler_params=_cparams(("arbitrary", "arbitrary")),
        name="peer_experts",
    )(h2t, u_b, vt_b, *route, x1, mods3, g_final.reshape(1, D_MODEL))


def _group_forward(x2, mods3, tm, rows_per_mod, w, attend):
    proj = in_projection(x2, mods3, w["g_norm1"], w["w_in"], tm, rows_per_mod)
    o_m, o_n = attend(proj)
    x1, h2 = mid_block(x2, o_m, o_n, w["w_out"], w["g_norm2"], mods3, min(tm, 256), rows_per_mod)
    route = peer_route(h2, w["peer_w_q"], w["peer_keys"], min(tm, 256))
    y = peer_experts(h2.T, w["peer_u"], w["peer_vt"], route, x1, mods3, w["g_final"], tm, PEER_EXPERT_TILE, rows_per_mod)
    return proj, y


def kernel(x_prompt, x_sample, cache_moba_kv, cache_nsa_cmp_kv, cache_nsa_slc_kv, state_nsa_win_kv, page_table,
           c_prompt, c_sample, w_ada, b_ada, g_norm1, w_in, cmp_pe_k, cmp_w1_k, cmp_w2_k, cmp_pe_v, cmp_w1_v,
           cmp_w2_v, w_out, g_norm2, peer_w_q, peer_keys, peer_u, peer_v, g_final):
    assert w_ada.shape[0] == 1, "single layer"
    batch, seq, _ = x_prompt.shape
    nseq, dec_seq, _ = x_sample.shape
    assert dec_seq == 1 and state_nsa_win_kv.shape[2] == NSA_WINDOW and seq >= NSA_WINDOW
    G, H, dh = NSA_KV_HEADS, MOBA_HEADS, HEAD_DIM

    c_all = jnp.concatenate([c_prompt, c_sample], axis=0)
    pad = (-c_all.shape[0]) % 8
    mods = ada_mods(jnp.pad(c_all, ((0, pad), (0, 0))), w_ada[0], b_ada[0])
    mods_p = mods[:batch].reshape(batch, 1, N_MOD * D_MODEL)
    mods_s = mods[batch:batch + nseq].reshape(1, nseq, N_MOD * D_MODEL)

    p_in = w_in.shape[2]
    pe2 = jnp.stack([cmp_pe_k[0], cmp_pe_v[0]])
    w12 = jnp.stack([cmp_w1_k[0], cmp_w1_v[0]]).astype(bf16)
    w22 = jnp.stack([cmp_w2_k[0], cmp_w2_v[0]]).astype(bf16)
    w = dict(
        g_norm1=g_norm1[0], g_norm2=g_norm2[0], g_final=g_final,
        w_in=jnp.pad(w_in[0], ((0, 0), (0, IN_COLS - p_in))).astype(bf16),
        w_out=w_out[0].astype(bf16),
        peer_w_q=peer_w_q[0].astype(bf16), peer_keys=peer_keys[0],
        peer_u=peer_u[0].astype(bf16), peer_vt=peer_v[0].T.astype(bf16),
    )

    def attend_prompt(proj):
        qm, kvm, qn, kvc, kvs, kvw, gates = proj[:7]
        ckv = compress_prompt(kvc, pe2, w12, w22, batch, seq)
        o_m = moba_prompt(qm, kvm, kvm[:, MOBA_W:].T, batch, seq)
        o_n = nsa_prompt(qn, gates, ckv[:, 0], jnp.swapaxes(ckv[:, 1], 2, 3),
                         kvs, kvs[:, NSA_KVW:].T, kvw, kvw[:, NSA_KVW:].T, batch, seq)
        return o_m, o_n

    win_out = []

    def attend_sample(proj):
        qm, kvm, qn, kvc, kvs, kvw, gates = proj[:7]
        o_m = moba_sample(qm.reshape(nseq, H, dh), kvm.reshape(nseq, 2, H, dh), cache_moba_kv.reshape(-1, H, dh), page_table)
        qn3 = qn.reshape(nseq, NSA_HEADS, dh)
        o_c, sel = nsa_sample_cmp(qn3, cache_nsa_cmp_kv.reshape(-1, dh), page_table, pe2,
                                  w12[0:1], w12[1:2], w22[0:1], w22[1:2])
        sel_flat = sel[:, :2 * NSA_SEL_TOPK].reshape(nseq, G * 2 * NSA_SEL_TOPK)
        gates3 = jnp.pad(gates[:, :NSA_HEADS * NSA_BRANCHES].reshape(nseq, NSA_HEADS, NSA_BRANCHES),
                         ((0, 0), (0, 0), (0, LANES - NSA_BRANCHES)))
        o_n, wout = nsa_sample_attn(qn3, gates3, o_c, kvs.reshape(nseq, KV_ROWS, dh), kvw.reshape(nseq, KV_ROWS, dh),
                                    cache_nsa_slc_kv.reshape(-1, dh), state_nsa_win_kv.reshape(-1, dh), page_table, sel_flat)
        win_out.append(wout)
        return o_m.reshape(nseq, MOBA_W), o_n.reshape(nseq, NSA_W)

    proj_p, y_p = _group_forward(x_prompt.reshape(batch * seq, D_MODEL), mods_p, 512, seq, w, attend_prompt)
    proj_s, y_s = _group_forward(x_sample.reshape(nseq, D_MODEL), mods_s, nseq, nseq, w, attend_sample)

    kv_p = lambda a, nh: a.reshape(1, batch, seq, 2, nh, dh)
    kv_s = lambda a, nh: a.reshape(1, nseq, 1, 2, nh, dh)
    return (y_p.reshape(batch, seq, D_MODEL), y_s.reshape(nseq, 1, D_MODEL),
            kv_p(proj_p[7], H), kv_s(proj_s[7], H),
            kv_p(proj_p[8], G), kv_s(proj_s[8], G),
            kv_p(proj_p[9], G), kv_s(proj_s[9], G),
            kv_p(proj_p[10], G)[:, :, seq - NSA_WINDOW:],
            win_out[0].reshape(state_nsa_win_kv.shape))
```
